```python
import math
import jax, jax.numpy as jnp
from jax import lax
import numpy as np


D_MODEL = 1024
BATCH = 8
SEQ = 2048
DEPTH = 4
DEC_BATCH = 128
DEC_SEQ = 8
PAST_LEN = 2048
PAGE_SIZE = 128

N_BRANCH = 4
H_A = 4
DH_A = 64
DV_A = 2 * DH_A
W_A = H_A * DV_A
Q_BLOCK = 128
ROPE_THETA = 10000.0
W_B = D_MODEL // 2
NB_B = 8
BW_B = W_B // NB_B
LRU_C = 8.0
CONV_W = 4
W_C = D_MODEL // 2
G_C = 4
CG_C = W_C // G_C
CHUNK_C = 128
H_D = 8
P_D = 64
W_D = H_D * P_D
G_D = 2
N_D = 128
SSD_CHUNK = 128
XBC_D = W_D + 2 * G_D * N_D
N_MEM = 256
XH = 4
XDH = 128
XW = XH * XDH
D_FF = ((8 * D_MODEL // 3 + 255) // 256) * 256
IN_SIZES = (2 * H_A * DH_A, 2 * H_A * DH_A, W_A, W_B, W_B, W_C, W_C, W_D, XBC_D, H_D, N_BRANCH * D_MODEL)
N_IN = sum(IN_SIZES)
IN_SPLITS = tuple(int(s) for s in np.cumsum(IN_SIZES)[:-1])
EPS = 1e-6

kernel_name = 'hybrid_gated_branch_decoder_step'


def rmsnorm(x, g):
    xf = x.astype(jnp.float32)
    y = xf * lax.rsqrt(jnp.mean(xf * xf, axis=-1, keepdims=True) + EPS)
    return (y * g.astype(jnp.float32)).astype(x.dtype)


def rope(x, pos):
    d = x.shape[-1]
    half = d // 2
    inv = ROPE_THETA ** (-jnp.arange(half, dtype=jnp.float32) / half)
    ang = pos.astype(jnp.float32)[:, None] * inv[None, :]
    cos = jnp.cos(ang)[None, :, None, :]
    sin = jnp.sin(ang)[None, :, None, :]
    xf = x.astype(jnp.float32)
    x1, x2 = xf[..., :half], xf[..., half:]
    return jnp.concatenate([x1 * cos - x2 * sin, x2 * cos + x1 * sin], axis=-1).astype(x.dtype)


def causal_conv(x, buf, w, b):
    L = x.shape[1]
    xx = jnp.concatenate([buf.astype(x.dtype), x], axis=1)
    y = b
    for t in range(CONV_W):
        y = y + w[t] * xx[:, t:t + L]
    return y.astype(x.dtype), xx[:, -(CONV_W - 1):]


def diff_attn_block(q, k, v, q_pos, k_pos, lam):
    b, lq = q.shape[0], q.shape[1]
    lk = k.shape[1]
    s = jnp.einsum('bqhd,bkhd->bhqk', q, k).astype(jnp.float32) * (DH_A ** -0.5)
    mask = k_pos[None, :] <= q_pos[:, None]
    s = jnp.where(mask[None, None], s, -jnp.inf)
    p = jax.nn.softmax(s, axis=-1).reshape(b, H_A, 2, lq, lk)
    a = p[:, :, 0] - lam * p[:, :, 1]
    return jnp.einsum('bhqk,bkhd->bqhd', a, v.astype(jnp.float32))


def diff_attention(q, k, v, q_pos, k_pos, lam):
    b, L = q.shape[0], q.shape[1]
    qb = min(Q_BLOCK, L)
    nb = L // qb
    qs = jnp.moveaxis(q.reshape((b, nb, qb) + q.shape[2:]), 1, 0)
    ps = q_pos.reshape(nb, qb)
    out = lax.map(lambda a: diff_attn_block(a[0], k, v, a[1], k_pos, lam), (qs, ps))
    return jnp.moveaxis(out, 0, 1).reshape(b, L, H_A, DV_A).astype(q.dtype)


def rg_lru(xc, h0, wa, ba, wx, bx, lam_param):
    b, L, W = xc.shape
    xf = xc.astype(jnp.float32)
    xb = xf.reshape(b, L, NB_B, BW_B)
    r = jax.nn.sigmoid(jnp.einsum('blhi,hij->blhj', xb, wa).reshape(b, L, W) + ba)
    i = jax.nn.sigmoid(jnp.einsum('blhi,hij->blhj', xb, wx).reshape(b, L, W) + bx)
    log_a = -LRU_C * r * jax.nn.softplus(-lam_param.astype(jnp.float32))
    a = jnp.exp(log_a)
    u = jnp.sqrt(-jnp.expm1(2.0 * log_a)) * (i * xf)
    u = u.at[:, 0].add(a[:, 0] * h0.astype(jnp.float32))

    def comb(c1, c2):
        a1, b1 = c1
        a2, b2 = c2
        return a1 * a2, a2 * b1 + b2

    _, h = lax.associative_scan(comb, (a, u), axis=1)
    return h.astype(xc.dtype), h[:, -1].astype(h0.dtype)


def chunk_mix(u, v, w_s, b_s):
    b, L, W = v.shape
    pad = (-L) % CHUNK_C
    nc = (L + pad) // CHUNK_C
    vp = jnp.pad(v, ((0, 0), (0, pad), (0, 0))).reshape(b, nc, CHUNK_C, G_C, CG_C)
    mask = jnp.tril(jnp.ones((CHUNK_C, CHUNK_C), dtype=bool))
    ws = jnp.where(mask[None], w_s, 0.0)
    mixed = jnp.einsum('gij,bcjgd->bcigd', ws, vp) + b_s.T[None, None, :, :, None]
    mixed = mixed.reshape(b, nc * CHUNK_C, W)[:, :L]
    return (u * mixed).astype(u.dtype)


def ssd(x, dt, A, Bm, Cm, h0):
    dtype = x.dtype
    b, L, H, P = x.shape
    G = Bm.shape[2]
    Q = min(SSD_CHUNK, L)
    pad = (-L) % Q
    nc = (L + pad) // Q

    def blk(t):
        t = jnp.pad(t.astype(jnp.float32), [(0, 0), (0, pad)] + [(0, 0)] * (t.ndim - 2))
        return t.reshape((b, nc, Q) + t.shape[2:])

    xq, dtq = blk(x), blk(dt)
    Bq = blk(jnp.repeat(Bm, H // G, axis=2))
    Cq = blk(jnp.repeat(Cm, H // G, axis=2))
    cum = jnp.cumsum(dtq * A, axis=2)
    causal = jnp.tril(jnp.ones((Q, Q), dtype=bool))
    seg = cum[:, :, :, None, :] - cum[:, :, None, :, :]
    decay = jnp.exp(jnp.where(causal[None, None, :, :, None], seg, -jnp.inf))
    xdt = xq * dtq[..., None]
    scores = jnp.einsum('bcihn,bcjhn->bcijh', Cq, Bq) * decay
    y_diag = jnp.einsum('bcijh,bcjhp->bcihp', scores, xdt)
    to_end = jnp.exp(cum[:, :, -1:, :] - cum)
    states = jnp.einsum('bcjhn,bcjh,bcjhp->bchpn', Bq, to_end, xdt)
    chunk_decay = jnp.exp(cum[:, :, -1, :])

    def step(h, inp):
        d, s = inp
        return d[:, :, None, None] * h + s, h

    h_final, h_prev = lax.scan(step, h0.astype(jnp.float32),
                               (jnp.moveaxis(chunk_decay, 1, 0), jnp.moveaxis(states, 1, 0)))
    h_prev = jnp.moveaxis(h_prev, 0, 1)
    y_off = jnp.einsum('bcihn,bchpn->bcihp', Cq, h_prev) * jnp.exp(cum)[..., None]
    y = (y_diag + y_off).reshape(b, nc * Q, H, P)[:, :L]
    return y.astype(dtype), h_final.astype(h0.dtype)


def layer(l, P, x, pos, past_k, past_v, lru_h0, lru_buf, ssm_h0, ssm_buf, mem_k, mem_v):
    b, L, _ = x.shape
    h = rmsnorm(x, P['norm_mix'][l])
    proj = h @ P['w_in'][l]
    q, k, v, xb, gb, uc, vc, zd, xbcd, dtd, gates = jnp.split(proj, IN_SPLITS, axis=-1)
    q = rope(rmsnorm(q.reshape(b, L, 2 * H_A, DH_A), P['q_norm'][l]), pos)
    k = rope(rmsnorm(k.reshape(b, L, 2 * H_A, DH_A), P['k_norm'][l]), pos)
    v = v.reshape(b, L, H_A, DV_A)
    if past_k is None:
        k_all, v_all = k, v
    else:
        k_all = jnp.concatenate([past_k.astype(k.dtype), k], axis=1)
        v_all = jnp.concatenate([past_v.astype(v.dtype), v], axis=1)
    k_pos = jnp.arange(k_all.shape[1])
    lam_init = 0.8 - 0.6 * math.exp(-0.3 * l)
    lq = P['lambda_qk'][l].astype(jnp.float32)
    lam = jnp.exp(jnp.sum(lq[0] * lq[1])) - jnp.exp(jnp.sum(lq[2] * lq[3])) + lam_init
    o = diff_attention(q, k_all, v_all, pos, k_pos, lam)
    y_a = (rmsnorm(o, P['subln'][l]) * (1.0 - lam_init)).reshape(b, L, W_A)
    xc, new_lru_buf = causal_conv(xb, lru_buf, P['lru_conv_w'][l], P['lru_conv_b'][l])
    hb, new_lru_h = rg_lru(xc, lru_h0, P['lru_wa'][l], P['lru_ba'][l], P['lru_wx'][l], P['lru_bx'][l], P['lru_lambda'][l])
    y_b = hb * jax.nn.gelu(gb)
    uc = jax.nn.gelu(uc)
    vc = rmsnorm(jax.nn.gelu(vc), P['sgu_norm'][l])
    y_c = chunk_mix(uc, vc, P['sgu_w'][l], P['sgu_b'][l])
    xbc, new_ssm_buf = causal_conv(xbcd, ssm_buf, P['ssd_conv_w'][l], P['ssd_conv_b'][l])
    xbc = jax.nn.silu(xbc)
    xs, bm, cm = jnp.split(xbc, [W_D, W_D + G_D * N_D], axis=-1)
    dt = jax.nn.softplus((dtd + P['ssd_dt_bias'][l]).astype(jnp.float32))
    a = -jnp.exp(P['ssd_a_log'][l].astype(jnp.float32))
    xs = xs.reshape(b, L, H_D, P_D)
    yd, new_ssm_h = ssd(xs, dt, a, bm.reshape(b, L, G_D, N_D), cm.reshape(b, L, G_D, N_D), ssm_h0)
    yd = (yd + P['ssd_d'][l][:, None] * xs).reshape(b, L, W_D)
    y_d = rmsnorm(yd * jax.nn.silu(zd), P['ssd_norm'][l])
    branches = jnp.stack([y_a, y_b, y_c, y_d], axis=2)
    g = jax.nn.sigmoid(gates.reshape(b, L, N_BRANCH, D_MODEL))
    proj_br = jnp.einsum('blkw,kwd->blkd', branches, P['w_branch'][l])
    x = x + jnp.sum(g * proj_br, axis=2) @ P['w_o'][l]
    h = rmsnorm(x, P['norm_cross'][l])
    qx = rmsnorm((h @ P['w_xq'][l]).reshape(b, L, XH, XDH), P['xq_norm'][l])
    s = jnp.einsum('blhd,bmhd->bhlm', qx, mem_k.astype(qx.dtype)).astype(jnp.float32) * (XDH ** -0.5)
    p = jax.nn.softmax(s, axis=-1).astype(x.dtype)
    ox = jnp.einsum('bhlm,bmhd->blhd', p, mem_v.astype(x.dtype)).reshape(b, L, XW)
    x = x + ox @ P['w_xo'][l]
    h = rmsnorm(x, P['norm_ffn'][l])
    gf, uf = jnp.split(h @ P['w_ffn_in'][l], 2, axis=-1)
    x = x + (jax.nn.silu(gf) * uf) @ P['w_ffn_out'][l]
    return x, (k, v, new_lru_h, new_lru_buf, new_ssm_h, new_ssm_buf, vc)


def setup_inputs(seed: int = 0) -> dict:
    key = jax.random.key(seed)
    ks = iter(jax.random.split(key, 64))

    def nrm(shape, scale):
        return jax.random.normal(next(ks), shape, jnp.float32) * scale

    def gain(shape):
        return 1.0 + nrm(shape, 0.05)

    n_pages = PAST_LEN // PAGE_SIZE
    n_used = DEC_BATCH * n_pages
    n_phys = n_used + max(1, n_used // 4)
    page_table = jax.random.permutation(next(ks), n_phys)[:n_used].reshape(DEC_BATCH, n_pages).astype(jnp.int32)

    a_c = jax.random.uniform(next(ks), (DEPTH, W_B), jnp.float32, minval=0.9, maxval=0.999)
    s_a = a_c ** (1.0 / LRU_C)
    lru_lambda = jnp.log(s_a) - jnp.log1p(-s_a)
    dt0 = jnp.exp(jax.random.uniform(next(ks), (DEPTH, H_D), jnp.float32, minval=math.log(1e-3), maxval=math.log(1e-1)))
    dt_bias = dt0 + jnp.log(-jnp.expm1(-dt0))
    a_log = jnp.log(jax.random.uniform(next(ks), (DEPTH, H_D), jnp.float32, minval=1.0, maxval=16.0))

    return {
        'x_prompt': nrm((BATCH, SEQ, D_MODEL), 1.0),
        'x_sample': nrm((DEC_BATCH, DEC_SEQ, D_MODEL), 1.0),
        'mem_prompt': nrm((BATCH, N_MEM, D_MODEL), 1.0),
        'cache_attn_k': nrm((DEPTH, n_phys, PAGE_SIZE, 2 * H_A, DH_A), 1.0),
        'cache_attn_v': nrm((DEPTH, n_phys, PAGE_SIZE, H_A, DV_A), 1.0),
        'page_table': page_table,
        'cache_mem_k': nrm((DEPTH, DEC_BATCH, N_MEM, XH, XDH), 1.0),
        'cache_mem_v': nrm((DEPTH, DEC_BATCH, N_MEM, XH, XDH), 1.0),
        'state_lru': nrm((DEPTH, DEC_BATCH, W_B), 0.5),
        'state_lru_conv': nrm((DEPTH, DEC_BATCH, CONV_W - 1, W_B), 1.0),
        'state_ssm': nrm((DEPTH, DEC_BATCH, H_D, P_D, N_D), 0.5),
        'state_ssm_conv': nrm((DEPTH, DEC_BATCH, CONV_W - 1, XBC_D), 1.0),
        'norm_mix': gain((DEPTH, D_MODEL)),
        'w_in': nrm((DEPTH, D_MODEL, N_IN), D_MODEL ** -0.5),
        'q_norm': gain((DEPTH, DH_A)),
        'k_norm': gain((DEPTH, DH_A)),
        'lambda_qk': nrm((DEPTH, 4, DH_A), 0.1),
        'subln': gain((DEPTH, H_A, DV_A)),
        'lru_conv_w': nrm((DEPTH, CONV_W, W_B), CONV_W ** -0.5),
        'lru_conv_b': nrm((DEPTH, W_B), 0.01),
        'lru_wa': nrm((DEPTH, NB_B, BW_B, BW_B), BW_B ** -0.5),
        'lru_ba': nrm((DEPTH, W_B), 0.01),
        'lru_wx': nrm((DEPTH, NB_B, BW_B, BW_B), BW_B ** -0.5),
        'lru_bx': nrm((DEPTH, W_B), 0.01),
        'lru_lambda': lru_lambda,
        'sgu_norm': gain((DEPTH, W_C)),
        'sgu_w': nrm((DEPTH, G_C, CHUNK_C, CHUNK_C), CHUNK_C ** -0.5),
        'sgu_b': 1.0 + nrm((DEPTH, G_C, CHUNK_C), 0.01),
        'ssd_conv_w': nrm((DEPTH, CONV_W, XBC_D), CONV_W ** -0.5),
        'ssd_conv_b': nrm((DEPTH, XBC_D), 0.01),
        'ssd_dt_bias': dt_bias,
        'ssd_a_log': a_log,
        'ssd_d': 1.0 + nrm((DEPTH, H_D), 0.05),
        'ssd_norm': gain((DEPTH, W_D)),
        'w_branch': nrm((DEPTH, N_BRANCH, W_A, D_MODEL), W_A ** -0.5),
        'w_o': nrm((DEPTH, D_MODEL, D_MODEL), D_MODEL ** -0.5),
        'norm_cross': gain((DEPTH, D_MODEL)),
        'w_xq': nrm((DEPTH, D_MODEL, XW), D_MODEL ** -0.5),
        'w_xk': nrm((DEPTH, D_MODEL, XW), D_MODEL ** -0.5),
        'w_xv': nrm((DEPTH, D_MODEL, XW), D_MODEL ** -0.5),
        'xq_norm': gain((DEPTH, XDH)),
        'xk_norm': gain((DEPTH, XDH)),
        'w_xo': nrm((DEPTH, XW, D_MODEL), XW ** -0.5),
        'norm_ffn': gain((DEPTH, D_MODEL)),
        'w_ffn_in': nrm((DEPTH, D_MODEL, 2 * D_FF), D_MODEL ** -0.5),
        'w_ffn_out': nrm((DEPTH, D_FF, D_MODEL), D_FF ** -0.5),
    }


def reference(x_prompt, x_sample, mem_prompt, cache_attn_k, cache_attn_v, page_table, cache_mem_k, cache_mem_v,
              state_lru, state_lru_conv, state_ssm, state_ssm_conv,
              norm_mix, w_in, q_norm, k_norm, lambda_qk, subln,
              lru_conv_w, lru_conv_b, lru_wa, lru_ba, lru_wx, lru_bx, lru_lambda,
              sgu_norm, sgu_w, sgu_b,
              ssd_conv_w, ssd_conv_b, ssd_dt_bias, ssd_a_log, ssd_d, ssd_norm,
              w_branch, w_o, norm_cross, w_xq, w_xk, w_xv, xq_norm, xk_norm, w_xo,
              norm_ffn, w_ffn_in, w_ffn_out):
    P = {'norm_mix': norm_mix, 'w_in': w_in, 'q_norm': q_norm, 'k_norm': k_norm, 'lambda_qk': lambda_qk,
         'subln': subln, 'lru_conv_w': lru_conv_w, 'lru_conv_b': lru_conv_b, 'lru_wa': lru_wa, 'lru_ba': lru_ba,
         'lru_wx': lru_wx, 'lru_bx': lru_bx, 'lru_lambda': lru_lambda, 'sgu_norm': sgu_norm, 'sgu_w': sgu_w,
         'sgu_b': sgu_b, 'ssd_conv_w': ssd_conv_w, 'ssd_conv_b': ssd_conv_b, 'ssd_dt_bias': ssd_dt_bias,
         'ssd_a_log': ssd_a_log, 'ssd_d': ssd_d, 'ssd_norm': ssd_norm, 'w_branch': w_branch, 'w_o': w_o,
         'norm_cross': norm_cross, 'w_xq': w_xq, 'xq_norm': xq_norm, 'w_xo': w_xo,
         'norm_ffn': norm_ffn, 'w_ffn_in': w_ffn_in, 'w_ffn_out': w_ffn_out}
    bp, lp = x_prompt.shape[0], x_prompt.shape[1]
    bs, ls = x_sample.shape[0], x_sample.shape[1]
    n_pages = page_table.shape[1]
    past_len = n_pages * PAGE_SIZE
    pos_p = jnp.arange(lp)
    pos_s = past_len + jnp.arange(ls)
    dt_p = x_prompt.dtype

    yp, ys = x_prompt, x_sample
    prompt_states, sample_states, mem_ks, mem_vs = [], [], [], []
    for l in range(DEPTH):
        mk = rmsnorm((mem_prompt @ w_xk[l]).reshape(bp, N_MEM, XH, XDH), xk_norm[l])
        mv = (mem_prompt @ w_xv[l]).reshape(bp, N_MEM, XH, XDH)
        yp, sp = layer(l, P, yp, pos_p, None, None,
                       jnp.zeros((bp, W_B), dt_p), jnp.zeros((bp, CONV_W - 1, W_B), dt_p),
                       jnp.zeros((bp, H_D, P_D, N_D), dt_p), jnp.zeros((bp, CONV_W - 1, XBC_D), dt_p),
                       mk, mv)
        prompt_states.append(sp[:6])
        mem_ks.append(mk)
        mem_vs.append(mv)
        past_k = cache_attn_k[l][page_table].reshape(bs, past_len, 2 * H_A, DH_A)
        past_v = cache_attn_v[l][page_table].reshape(bs, past_len, H_A, DV_A)
        ys, ss = layer(l, P, ys, pos_s, past_k, past_v,
                       state_lru[l], state_lru_conv[l], state_ssm[l], state_ssm_conv[l],
                       cache_mem_k[l], cache_mem_v[l])
        sample_states.append(ss)

    pk, pv, plru, plconv, pssm, psconv = [jnp.stack(t) for t in zip(*prompt_states)]
    sk, sv, slru, slconv, sssm, ssconv, schunk = [jnp.stack(t) for t in zip(*sample_states)]
    pmk = jnp.stack(mem_ks)
    pmv = jnp.stack(mem_vs)
    return (yp, ys, pk, pv, pmk, pmv, plru, plconv, pssm, psconv, sk, sv, slru, slconv, sssm, ssconv, schunk)
```

```python
import functools
import math

import jax
import jax.numpy as jnp
from jax import lax
from jax.experimental import pallas as pl
from jax.experimental.pallas import tpu as pltpu

f32 = jnp.float32
bf16 = jnp.bfloat16

D_MODEL = 1024
DEPTH = 4
PAGE_SIZE = 128
H_A = 4
DH_A = 64
DV_A = 128
W_A = 512
ROPE_THETA = 10000.0
W_B = 512
NB_B = 8
LRU_C = 8.0
CONV_W = 4
W_C = 512
G_C = 4
CHUNK_C = 128
H_D = 8
P_D = 64
W_D = 512
G_D = 2
N_D = 128
SSD_CHUNK = 128
XBC_D = 1024
N_MEM = 256
XH = 4
XDH = 128
XW = 512
D_FF = 2816
EPS = 1e-6

LANES = 128
SUBLANES = 8
VMEM_LIMIT = 48 * 1024 * 1024

N_MAIN = 9216
N_GATE_BLOCKS = 8
COL_Q, COL_K, COL_V, COL_XB, COL_GB, COL_UC, COL_VC, COL_ZD = range(N_GATE_BLOCKS, N_GATE_BLOCKS + 8)
COL_XBC = 8


def _cparams(sem):
    return pltpu.CompilerParams(dimension_semantics=sem, vmem_limit_bytes=VMEM_LIMIT)


def _dot(a, b):
    return jnp.dot(a, b, preferred_element_type=f32)


def _dot_nt(a, b):
    return lax.dot_general(a, b, (((1,), (1,)), ((), ())), preferred_element_type=f32)


def _dot_tn(a, b):
    return lax.dot_general(a, b, (((0,), (0,)), ((), ())), preferred_element_type=f32)


def _rms(x, g):
    ms = jnp.mean(x * x, axis=-1, keepdims=True)
    return x * lax.rsqrt(ms + EPS) * g


def _softplus(x):
    return jnp.maximum(x, 0.0) + jnp.log1p(jnp.exp(-jnp.abs(x)))


def _in_proj_body(x_ref, g_ref, w_ref, wdt_ref, proj_ref, dt_ref, hn_ref):
    j = pl.program_id(1)

    @pl.when(j == 0)
    def _():
        hn = _rms(x_ref[...], g_ref[...]).astype(bf16)
        hn_ref[...] = hn
        dt_ref[...] = _dot(hn, wdt_ref[...])

    acc = _dot(hn_ref[...], w_ref[...])

    @pl.when(j < N_GATE_BLOCKS)
    def _():
        proj_ref[...] = jax.nn.sigmoid(acc)

    @pl.when(((j >= COL_Q) & (j <= COL_XB)) | (j > COL_ZD))
    def _():
        proj_ref[...] = acc

    @pl.when((j >= COL_GB) & (j <= COL_VC))
    def _():
        proj_ref[...] = jax.nn.gelu(acc)

    @pl.when(j == COL_ZD)
    def _():
        proj_ref[...] = acc * jax.nn.sigmoid(acc)


def _in_proj(x, g, w_main, w_dt):
    t = x.shape[0]
    tm = min(t, 1024)
    tn = 512
    return pl.pallas_call(
        _in_proj_body,
        grid=(t // tm, N_MAIN // tn),
        in_specs=[
            pl.BlockSpec((tm, D_MODEL), lambda i, j: (i, 0)),
            pl.BlockSpec((1, D_MODEL), lambda i, j: (0, 0)),
            pl.BlockSpec((D_MODEL, tn), lambda i, j: (0, j)),
            pl.BlockSpec((D_MODEL, LANES), lambda i, j: (0, 0)),
        ],
        out_specs=[
            pl.BlockSpec((tm, tn), lambda i, j: (i, j)),
            pl.BlockSpec((tm, LANES), lambda i, j: (i, 0)),
        ],
        out_shape=[jax.ShapeDtypeStruct((t, N_MAIN), f32), jax.ShapeDtypeStruct((t, LANES), f32)],
        scratch_shapes=[pltpu.VMEM((tm, D_MODEL), bf16)],
        compiler_params=_cparams(("parallel", "arbitrary")),
        name="in_proj",
    )(x, g, w_main, w_dt)


def _head_norm_rope(x, gain, cos, sin, bd):
    sq = x * x
    hi = sq.astype(bf16)
    lo = (sq - hi.astype(f32)).astype(bf16)
    ms = _dot(hi, bd) + _dot(lo, bd)
    y = x * lax.rsqrt(ms + EPS) * gain
    lane = lax.broadcasted_iota(jnp.int32, y.shape, 1)
    first_half = (lane % DH_A) < (DH_A // 2)
    swapped = jnp.where(first_half, pltpu.roll(y, LANES - DH_A // 2, 1), pltpu.roll(y, DH_A // 2, 1))
    return y * cos + swapped * sin


def _qk_prep_body(q_ref, k_ref, v_ref, cos_ref, sin_ref, qg_ref, kg_ref, bd_ref, qo_ref, ko_ref, kb_ref, vb_ref):
    bd = bd_ref[...]
    for c in range(W_A // LANES):
        sl = slice(c * LANES, (c + 1) * LANES)
        cos = cos_ref[:, sl]
        sin = sin_ref[:, sl]
        q = _head_norm_rope(q_ref[:, sl], qg_ref[...], cos, sin, bd)
        k = _head_norm_rope(k_ref[:, sl], kg_ref[...], cos, sin, bd)
        qo_ref[:, sl] = (q * (DH_A ** -0.5)).astype(qo_ref.dtype)
        ko_ref[:, sl] = k
        kb_ref[:, sl] = k.astype(kb_ref.dtype)
    vb_ref[...] = v_ref[...].astype(vb_ref.dtype)


def _qk_prep(proj, cos, sin, qg, kg, bd, act_dtype):
    t = proj.shape[0]
    p = cos.shape[0]
    tm = min(t, p, 512)
    npos = p // tm
    row = lambda c: pl.BlockSpec((tm, 512), lambda i, c=c: (i, c))
    tab = pl.BlockSpec((tm, 512), lambda i: (i % npos, 0))
    small = lambda shape: pl.BlockSpec(shape, lambda i: (0, 0))
    return pl.pallas_call(
        _qk_prep_body,
        grid=(t // tm,),
        in_specs=[row(COL_Q), row(COL_K), row(COL_V), tab, tab, small((1, LANES)), small((1, LANES)),
                  small((LANES, LANES))],
        out_specs=[row(0)] * 4,
        out_shape=[jax.ShapeDtypeStruct((t, 512), act_dtype), jax.ShapeDtypeStruct((t, 512), f32),
                   jax.ShapeDtypeStruct((t, 512), act_dtype), jax.ShapeDtypeStruct((t, 512), act_dtype)],
        compiler_params=_cparams(("parallel",)),
        name="qk_prep",
    )(proj, proj, proj, cos, sin, qg, kg, bd)


def _lambda_value(lq, lam_init):
    a = jnp.sum(lq[0:1] * lq[1:2], axis=-1, keepdims=True)
    b = jnp.sum(lq[2:3] * lq[3:4], axis=-1, keepdims=True)
    return jnp.exp(a) - jnp.exp(b) + lam_init


def _stack_pair(qp):
    lane = lax.broadcasted_iota(jnp.int32, qp.shape, 1)
    zero = jnp.zeros_like(qp)
    return jnp.concatenate([jnp.where(lane < DH_A, qp, zero), jnp.where(lane >= DH_A, qp, zero)], axis=0)


def _diff_finish(acc, l, lam, sub, lam_init, tq):
    o = acc[:tq] / l[:tq] - lam * (acc[tq:] / l[tq:])
    return _rms(o, sub) * (1.0 - lam_init)


def _attn_prompt_body(lq_ref, q_ref, k_ref, v_ref, sub_ref, o_ref, *, lam_init, tq):
    qi = pl.program_id(1)
    lam = _lambda_value(lq_ref[...], lam_init)
    r = lax.broadcasted_iota(jnp.int32, (2 * tq, tq), 0)
    r = jnp.where(r >= tq, r - tq, r)
    c = lax.broadcasted_iota(jnp.int32, (2 * tq, tq), 1)
    diag = c <= r
    for h in range(H_A):
        sl = slice(h * LANES, (h + 1) * LANES)
        qq = _stack_pair(q_ref[:, sl])

        def step(kv, carry, masked):
            m, l, acc = carry
            rows = pl.ds(pl.multiple_of(kv * tq, tq), tq)
            s = _dot_nt(qq, k_ref[rows, sl])
            if masked:
                s = jnp.where(diag, s, -jnp.inf)
            m_new = jnp.maximum(m, jnp.max(s, axis=-1, keepdims=True))
            alpha = jnp.exp(m - m_new)
            p = jnp.exp(s - m_new)
            l = alpha * l + jnp.sum(p, axis=-1, keepdims=True)
            acc = alpha * acc + _dot(p.astype(bf16), v_ref[rows, sl])
            return m_new, l, acc

        init = (jnp.full((2 * tq, 1), -jnp.inf, f32), jnp.zeros((2 * tq, 1), f32), jnp.zeros((2 * tq, LANES), f32))
        carry = lax.fori_loop(0, qi, functools.partial(step, masked=False), init)
        _, l, acc = step(qi, carry, True)
        o_ref[:, sl] = _diff_finish(acc, l, lam, sub_ref[:, sl], lam_init, tq)


def _attn_prompt(lq, q, k, v, sub, lam_init, b, seq):
    tq = min(seq, 256)
    nq = seq // tq
    return pl.pallas_call(
        functools.partial(_attn_prompt_body, lam_init=lam_init, tq=tq),
        grid=(b, nq),
        in_specs=[
            pl.BlockSpec((4, DH_A), lambda i, j: (0, 0)),
            pl.BlockSpec((tq, 512), lambda i, j: (i * nq + j, 0)),
            pl.BlockSpec((seq, 512), lambda i, j: (i, 0)),
            pl.BlockSpec((seq, 512), lambda i, j: (i, 0)),
            pl.BlockSpec((1, 512), lambda i, j: (0, 0)),
        ],
        out_specs=pl.BlockSpec((tq, 512), lambda i, j: (i * nq + j, 0)),
        out_shape=jax.ShapeDtypeStruct((b * seq, 512), f32),
        compiler_params=_cparams(("parallel", "arbitrary")),
        name="attn_prompt",
    )(lq, q, k, v, sub)


def _attn_sample_body(pt_ref, lq_ref, q_ref, kn_ref, vn_ref, sub_ref, *rest, lam_init, n_pages, ls):
    del pt_ref
    k_refs = rest[:n_pages]
    v_refs = rest[n_pages:2 * n_pages]
    o_ref = rest[2 * n_pages]
    lam = _lambda_value(lq_ref[...], lam_init)
    pad = jnp.zeros((PAGE_SIZE - ls, LANES), f32)
    r = lax.broadcasted_iota(jnp.int32, (2 * ls, PAGE_SIZE), 0)
    r = jnp.where(r >= ls, r - ls, r)
    c = lax.broadcasted_iota(jnp.int32, (2 * ls, PAGE_SIZE), 1)
    new_mask = c <= r
    for h in range(H_A):
        sl = slice(h * LANES, (h + 1) * LANES)
        qq = _stack_pair(q_ref[:, sl]).astype(bf16)
        kt = jnp.concatenate([k_refs[p][2 * h:2 * h + 2].reshape(LANES, PAGE_SIZE) for p in range(n_pages)], axis=1)
        s_past = _dot(qq, kt.astype(bf16))
        k_new = jnp.concatenate([kn_ref[:, sl], pad], axis=0).astype(bf16)
        s_new = jnp.where(new_mask, _dot_nt(qq, k_new), -jnp.inf)
        m = jnp.maximum(jnp.max(s_past, axis=-1, keepdims=True), jnp.max(s_new, axis=-1, keepdims=True))
        p_past = jnp.exp(s_past - m)
        p_new = jnp.exp(s_new - m)
        l = jnp.sum(p_past, axis=-1, keepdims=True) + jnp.sum(p_new, axis=-1, keepdims=True)
        v_past = jnp.concatenate([v_refs[p][pl.ds(h, PAGE_SIZE, stride=H_A), :] for p in range(n_pages)], axis=0)
        v_new = jnp.concatenate([vn_ref[:, sl], pad], axis=0).astype(bf16)
        acc = _dot(p_past.astype(bf16), v_past.astype(bf16)) + _dot(p_new.astype(bf16), v_new)
        o_ref[:, sl] = _diff_finish(acc, l, lam, sub_ref[:, sl], lam_init, ls)


def _attn_sample(page_table, lq, q, k_new, v_new, sub, cache_kt, cache_v2, layer, lam_init, b, ls):
    n_pages = page_table.shape[1]
    row = pl.BlockSpec((ls, 512), lambda i, pt: (i, 0))
    k_specs = [pl.BlockSpec((None, None, 2 * H_A, DH_A, PAGE_SIZE), lambda i, pt, p=p: (layer, pt[i, p], 0, 0, 0))
               for p in range(n_pages)]
    v_specs = [pl.BlockSpec((None, None, PAGE_SIZE * H_A, DV_A), lambda i, pt, p=p: (layer, pt[i, p], 0, 0))
               for p in range(n_pages)]
    grid_spec = pltpu.PrefetchScalarGridSpec(
        num_scalar_prefetch=1,
        grid=(b,),
        in_specs=[pl.BlockSpec((4, DH_A), lambda i, pt: (0, 0)), row, row, row,
                  pl.BlockSpec((1, 512), lambda i, pt: (0, 0))] + k_specs + v_specs,
        out_specs=row,
    )
    return pl.pallas_call(
        functools.partial(_attn_sample_body, lam_init=lam_init, n_pages=n_pages, ls=ls),
        grid_spec=grid_spec,
        out_shape=jax.ShapeDtypeStruct((b * ls, 512), f32),
        compiler_params=_cparams(("parallel",)),
        name="attn_sample",
    )(page_table, lq, q, k_new, v_new, sub, *([cache_kt] * n_pages), *([cache_v2] * n_pages))


def _conv_tile(x_ref, buf_ref, w_ref, b_ref, xx_ref, first, tt):
    @pl.when(first)
    def _():
        xx_ref[SUBLANES - (CONV_W - 1):SUBLANES, :] = buf_ref[...]

    @pl.when(jnp.logical_not(first))
    def _():
        xx_ref[SUBLANES - (CONV_W - 1):SUBLANES, :] = xx_ref[tt + SUBLANES - (CONV_W - 1):tt + SUBLANES, :]

    xx_ref[SUBLANES:SUBLANES + tt, :] = x_ref[...]
    y = b_ref[...] + w_ref[0:1, :] * xx_ref[SUBLANES - 3:SUBLANES - 3 + tt, :]
    for j in range(1, CONV_W):
        y = y + w_ref[j:j + 1, :] * xx_ref[SUBLANES - 3 + j:SUBLANES - 3 + j + tt, :]
    return y


def _lru_body(x_ref, gate_ref, buf_ref, h0_ref, cw_ref, cb_ref, wa_ref, ba_ref, wx_ref, bx_ref, lam_ref,
              y_ref, hf_ref, xx_ref, a_ref, u_ref, h_ref, *, tt):
    ti = pl.program_id(1)
    xc = _conv_tile(x_ref, buf_ref, cw_ref, cb_ref, xx_ref, ti == 0, tt)
    xcb = xc.astype(bf16)
    r = jax.nn.sigmoid(_dot(xcb, wa_ref[...]) + ba_ref[...])
    i = jax.nn.sigmoid(_dot(xcb, wx_ref[...]) + bx_ref[...])
    log_a = (-LRU_C) * r * _softplus(-lam_ref[...])
    a_ref[...] = jnp.exp(log_a)
    th = jnp.tanh(log_a)
    u_ref[...] = jnp.sqrt(-2.0 * th / (1.0 - th)) * (i * xc)

    @pl.when(ti == 0)
    def _():
        h_ref[...] = h0_ref[...]

    row = lax.broadcasted_iota(jnp.int32, (SUBLANES, W_B), 0)

    def tile(k, h):
        rows = pl.ds(pl.multiple_of(k * SUBLANES, SUBLANES), SUBLANES)
        a = a_ref[rows, :]
        u = u_ref[rows, :]
        for s in (1, 2, 4):
            keep = row >= s
            u = jnp.where(keep, a * pltpu.roll(u, s, 0) + u, u)
            a = jnp.where(keep, a * pltpu.roll(a, s, 0), a)
        hh = a * h + u
        y_ref[rows, :] = hh * gate_ref[rows, :]
        return hh[SUBLANES - 1:SUBLANES, :]

    h = lax.fori_loop(0, tt // SUBLANES, tile, h_ref[...])
    h_ref[...] = h
    hf_ref[...] = h


def _lru(proj, buf, h0, cw, cb, wa, ba, wx, bx, lam, b, seq):
    tt = min(seq, 512)
    nt = seq // tt
    small = lambda shape: pl.BlockSpec(shape, lambda i, j: (0,) * len(shape))
    return pl.pallas_call(
        functools.partial(_lru_body, tt=tt),
        grid=(b, nt),
        in_specs=[
            pl.BlockSpec((tt, 512), lambda i, j: (i * nt + j, COL_XB)),
            pl.BlockSpec((tt, 512), lambda i, j: (i * nt + j, COL_GB)),
            pl.BlockSpec((None, CONV_W - 1, W_B), lambda i, j: (i, 0, 0)),
            pl.BlockSpec((None, 1, W_B), lambda i, j: (i, 0, 0)),
            small((CONV_W, W_B)), small((1, W_B)), small((W_B, W_B)), small((1, W_B)), small((W_B, W_B)),
            small((1, W_B)), small((1, W_B)),
        ],
        out_specs=[
            pl.BlockSpec((tt, 512), lambda i, j: (i * nt + j, 0)),
            pl.BlockSpec((None, 1, W_B), lambda i, j: (i, 0, 0)),
        ],
        out_shape=[jax.ShapeDtypeStruct((b * seq, W_B), f32), jax.ShapeDtypeStruct((b, 1, W_B), f32)],
        scratch_shapes=[pltpu.VMEM((tt + SUBLANES, W_B), f32), pltpu.VMEM((tt, W_B), f32), pltpu.VMEM((tt, W_B), f32),
                        pltpu.VMEM((1, W_B), f32)],
        compiler_params=_cparams(("parallel", "arbitrary")),
        name="lru",
    )(proj, proj, buf, h0, cw, cb, wa, ba, wx, bx, lam)


def _cmix_body(u_ref, v_ref, g_ref, w_ref, b_ref, y_ref, vn_ref, *, n_chunks):
    vn = _rms(v_ref[...], g_ref[...])
    vn_ref[...] = vn
    r = lax.broadcasted_iota(jnp.int32, (CHUNK_C, CHUNK_C), 0)
    c = lax.broadcasted_iota(jnp.int32, (CHUNK_C, CHUNK_C), 1)
    for g in range(G_C):
        sl = slice(g * LANES, (g + 1) * LANES)
        wg = jnp.where(c <= r, w_ref[g], 0.0).astype(bf16)
        bias = b_ref[:, g:g + 1]
        for ch in range(n_chunks):
            rows = slice(ch * CHUNK_C, (ch + 1) * CHUNK_C)
            mixed = _dot(wg, vn[rows, sl].astype(bf16)) + bias
            y_ref[rows, sl] = u_ref[rows, sl] * mixed


def _cmix(proj, g, w, bias):
    t = proj.shape[0]
    tc = min(t, 512)
    return pl.pallas_call(
        functools.partial(_cmix_body, n_chunks=tc // CHUNK_C),
        grid=(t // tc,),
        in_specs=[
            pl.BlockSpec((tc, 512), lambda i: (i, COL_UC)),
            pl.BlockSpec((tc, 512), lambda i: (i, COL_VC)),
            pl.BlockSpec((1, W_C), lambda i: (0, 0)),
            pl.BlockSpec((G_C, CHUNK_C, CHUNK_C), lambda i: (0, 0, 0)),
            pl.BlockSpec((CHUNK_C, G_C), lambda i: (0, 0)),
        ],
        out_specs=[pl.BlockSpec((tc, 512), lambda i: (i, 0))] * 2,
        out_shape=[jax.ShapeDtypeStruct((t, W_C), f32)] * 2,
        compiler_params=_cparams(("parallel",)),
        name="cmix",
    )(proj, proj, g, w, bias)


def _ssd_body(xbc_ref, z_ref, dt_ref, buf_ref, h0_ref, cw_ref, cb_ref, dtb_ref, alog_ref, d_ref, g_ref,
              y_ref, hf_ref, xx_ref, st_ref, *, rows_in, valid):
    ci = pl.program_id(1)
    q = SSD_CHUNK

    @pl.when(ci == 0)
    def _():
        st_ref[...] = h0_ref[...]

    xbc = _conv_tile(xbc_ref, buf_ref, cw_ref, cb_ref, xx_ref, ci == 0, rows_in)
    xbc = xbc * jax.nn.sigmoid(xbc)
    dt = _softplus(dt_ref[...] + dtb_ref[...])
    z = z_ref[...]
    if rows_in < q:
        padr = lambda a: jnp.concatenate([a, jnp.zeros((q - rows_in, a.shape[1]), f32)], axis=0)
        xbc, dt, z = padr(xbc), padr(dt), padr(z)
    xs = xbc[:, :W_D]
    a_neg = -jnp.exp(alog_ref[...])
    ri = lax.broadcasted_iota(jnp.int32, (q, q), 0)
    cj = lax.broadcasted_iota(jnp.int32, (q, q), 1)
    causal = cj <= ri
    tril = jnp.where(causal, 1.0, 0.0).astype(f32)
    cum = jnp.dot(tril, dt * a_neg, preferred_element_type=f32, precision=lax.Precision.HIGHEST)
    cum_t = cum.T
    lane = lax.broadcasted_iota(jnp.int32, (q, LANES), 1)
    low = lane < P_D
    rlow = lax.broadcasted_iota(jnp.int32, (LANES, N_D), 0) < P_D
    y_parts = []
    for g in range(G_D):
        bm = xbc[:, W_D + g * N_D:W_D + (g + 1) * N_D].astype(bf16)
        cm = xbc[:, W_D + (G_D + g) * N_D:W_D + (G_D + g + 1) * N_D].astype(bf16)
        scores = _dot_nt(cm, bm)
        for k in range(H_D // G_D // 2):
            h0 = g * (H_D // G_D) + 2 * k
            sl = slice(h0 * P_D, (h0 + 2) * P_D)
            col = lambda a, h: a[:, h:h + 1]
            pair = lambda a: jnp.where(low, col(a, h0), col(a, h0 + 1))
            xdt = xs[:, sl] * pair(dt)
            xdt_b = xdt.astype(bf16)
            yd = []
            for h in (h0, h0 + 1):
                seg = col(cum, h) - cum_t[h:h + 1, :]
                m = scores * jnp.exp(jnp.where(causal, seg, -jnp.inf))
                yd.append(_dot(m.astype(bf16), xdt_b))
            st = st_ref[sl, :]
            y_off = _dot_nt(cm, st.astype(bf16)) * jnp.exp(pair(cum))
            y_parts.append(jnp.where(low, yd[0], yd[1]) + y_off + d_ref[:, sl] * xs[:, sl])
            last = cum[q - 1:q, :]
            xw = xdt * jnp.exp(pair(last - cum))
            dec = jnp.exp(jnp.where(rlow, last[:, h0:h0 + 1], last[:, h0 + 1:h0 + 2]))
            st_ref[sl, :] = dec * st + _dot_tn(xw.astype(bf16), bm)
    y = jnp.concatenate(y_parts, axis=1) * z
    y_ref[...] = _rms(y, g_ref[...])[:rows_in]
    hf_ref[...] = st_ref[...]


def _ssd(proj, dt_raw, buf, h0, cw, cb, dtb, alog, dvec, g, b, seq):
    rows = min(seq, SSD_CHUNK)
    nc = seq // rows
    small = lambda shape: pl.BlockSpec(shape, lambda i, j: (0,) * len(shape))
    return pl.pallas_call(
        functools.partial(_ssd_body, rows_in=rows, valid=rows),
        grid=(b, nc),
        in_specs=[
            pl.BlockSpec((rows, XBC_D), lambda i, j: (i * nc + j, COL_XBC)),
            pl.BlockSpec((rows, 512), lambda i, j: (i * nc + j, COL_ZD)),
            pl.BlockSpec((rows, LANES), lambda i, j: (i * nc + j, 0)),
            pl.BlockSpec((None, CONV_W - 1, XBC_D), lambda i, j: (i, 0, 0)),
            pl.BlockSpec((None, H_D * P_D, N_D), lambda i, j: (i, 0, 0)),
            small((CONV_W, XBC_D)), small((1, XBC_D)), small((1, LANES)), small((1, LANES)), small((1, W_D)),
            small((1, W_D)),
        ],
        out_specs=[
            pl.BlockSpec((rows, W_D), lambda i, j: (i * nc + j, 0)),
            pl.BlockSpec((None, H_D * P_D, N_D), lambda i, j: (i, 0, 0)),
        ],
        out_shape=[jax.ShapeDtypeStruct((b * seq, W_D), f32), jax.ShapeDtypeStruct((b, H_D * P_D, N_D), f32)],
        scratch_shapes=[pltpu.VMEM((rows + SUBLANES, XBC_D), f32), pltpu.VMEM((H_D * P_D, N_D), f32)],
        compiler_params=_cparams(("parallel", "arbitrary")),
        name="ssd",
    )(proj, proj, dt_raw, buf, h0, cw, cb, dtb, alog, dvec, g)


def _merge_body(x_ref, ya_ref, yb_ref, yc_ref, yd_ref, gates_ref, wb_ref, wo_ref, o_ref):
    m = None
    for k, y_ref in enumerate((ya_ref, yb_ref, yc_ref, yd_ref)):
        t = gates_ref[:, k * D_MODEL:(k + 1) * D_MODEL] * _dot(y_ref[...].astype(bf16), wb_ref[k])
        m = t if m is None else m + t
    o_ref[...] = x_ref[...] + _dot(m.astype(bf16), wo_ref[...])


def _merge(x, ya, yb, yc, yd, proj, wb, wo):
    t = x.shape[0]
    tm = min(t, 512)
    row = lambda w, c=0: pl.BlockSpec((tm, w), lambda i, c=c: (i, c))
    return pl.pallas_call(
        _merge_body,
        grid=(t // tm,),
        in_specs=[row(D_MODEL), row(512), row(512), row(512), row(512),
                  pl.BlockSpec((tm, 4 * D_MODEL), lambda i: (i, 0)),
                  pl.BlockSpec((4, 512, D_MODEL), lambda i: (0, 0, 0)),
                  pl.BlockSpec((D_MODEL, D_MODEL), lambda i: (0, 0))],
        out_specs=row(D_MODEL),
        out_shape=jax.ShapeDtypeStruct((t, D_MODEL), f32),
        compiler_params=_cparams(("parallel",)),
        name="merge",
    )(x, ya, yb, yc, yd, proj, wb, wo)


def _memkv_body(m_ref, wk_ref, wv_ref, g_ref, k_ref, v_ref):
    mb = m_ref[...].astype(bf16)
    kk = _dot(mb, wk_ref[...])
    for h in range(XH):
        sl = slice(h * XDH, (h + 1) * XDH)
        k_ref[:, sl] = _rms(kk[:, sl], g_ref[...])
    v_ref[...] = _dot(mb, wv_ref[...])


def _memkv(mem, wk, wv, g):
    r = mem.shape[0]
    tm = min(r, 512)
    return pl.pallas_call(
        _memkv_body,
        grid=(DEPTH, r // tm),
        in_specs=[pl.BlockSpec((tm, D_MODEL), lambda l, i: (i, 0)),
                  pl.BlockSpec((None, D_MODEL, XW), lambda l, i: (l, 0, 0)),
                  pl.BlockSpec((None, D_MODEL, XW), lambda l, i: (l, 0, 0)),
                  pl.BlockSpec((None, 1, XDH), lambda l, i: (l, 0, 0))],
        out_specs=[pl.BlockSpec((None, tm, XW), lambda l, i: (l, i, 0))] * 2,
        out_shape=[jax.ShapeDtypeStruct((DEPTH, r, XW), f32)] * 2,
        compiler_params=_cparams(("parallel", "parallel")),
        name="memkv",
    )(mem, wk, wv, g)


def _cross_body(x_ref, g_ref, wq_ref, qg_ref, k_ref, v_ref, wo_ref, o_ref, *, n_seq, rows, tall):
    x = x_ref[...]
    qx = _dot(_rms(x, g_ref[...]).astype(bf16), wq_ref[...])
    heads = []
    for h in range(XH):
        sl = slice(h * XDH, (h + 1) * XDH)
        qn = _rms(qx[:, sl], qg_ref[...]).astype(bf16)
        outs = []
        for s in range(n_seq):
            if tall:
                kh = k_ref[s, pl.ds(h, N_MEM, stride=XH), :]
                vh = v_ref[s, pl.ds(h, N_MEM, stride=XH), :]
            else:
                kh = k_ref[s, :, sl]
                vh = v_ref[s, :, sl]
            sc = _dot_nt(qn[s * rows:(s + 1) * rows], kh.astype(bf16)) * (XDH ** -0.5)
            p = jnp.exp(sc - jnp.max(sc, axis=-1, keepdims=True))
            p = p / jnp.sum(p, axis=-1, keepdims=True)
            outs.append(_dot(p.astype(bf16), vh.astype(bf16)))
        heads.append(outs[0] if n_seq == 1 else jnp.concatenate(outs, axis=0))
    ox = jnp.concatenate(heads, axis=1)
    o_ref[...] = x + _dot(ox.astype(bf16), wo_ref[...])


def _cross(x, g, wq, qg, mem_k, mem_v, wo, seq, tall):
    t = x.shape[0]
    if seq >= 512:
        n_seq, rows = 1, 512
        per = seq // rows
        mem_idx = lambda i: (i // per, 0, 0)
    else:
        n_seq, rows = 8, seq
        mem_idx = lambda i: (i, 0, 0)
    tm = n_seq * rows
    small = lambda shape: pl.BlockSpec(shape, lambda i: (0,) * len(shape))
    mem_block = (n_seq,) + tuple(mem_k.shape[1:])
    return pl.pallas_call(
        functools.partial(_cross_body, n_seq=n_seq, rows=rows, tall=tall),
        grid=(t // tm,),
        in_specs=[pl.BlockSpec((tm, D_MODEL), lambda i: (i, 0)), small((1, D_MODEL)), small((D_MODEL, XW)),
                  small((1, XDH)), pl.BlockSpec(mem_block, mem_idx), pl.BlockSpec(mem_block, mem_idx),
                  small((XW, D_MODEL))],
        out_specs=pl.BlockSpec((tm, D_MODEL), lambda i: (i, 0)),
        out_shape=jax.ShapeDtypeStruct((t, D_MODEL), f32),
        compiler_params=_cparams(("parallel",)),
        name="cross",
    )(x, g, wq, qg, mem_k, mem_v, wo)


def _ffn_body(x_ref, g_ref, wg_ref, wu_ref, wo_ref, o_ref, hn_ref, acc_ref):
    j = pl.program_id(1)

    @pl.when(j == 0)
    def _():
        hn_ref[...] = _rms(x_ref[...], g_ref[...]).astype(bf16)
        acc_ref[...] = jnp.zeros_like(acc_ref)

    hn = hn_ref[...]
    gate = _dot(hn, wg_ref[...])
    up = _dot(hn, wu_ref[...])
    act = (gate * jax.nn.sigmoid(gate) * up).astype(bf16)
    acc_ref[...] += _dot(act, wo_ref[...])

    @pl.when(j == pl.num_programs(1) - 1)
    def _():
        o_ref[...] = x_ref[...] + acc_ref[...]


def _ffn(x, g, w_in, w_out):
    t = x.shape[0]
    tm = min(t, 1024)
    tf = 256
    nf = D_FF // tf
    return pl.pallas_call(
        _ffn_body,
        grid=(t // tm, nf),
        in_specs=[pl.BlockSpec((tm, D_MODEL), lambda i, j: (i, 0)),
                  pl.BlockSpec((1, D_MODEL), lambda i, j: (0, 0)),
                  pl.BlockSpec((D_MODEL, tf), lambda i, j: (0, j)),
                  pl.BlockSpec((D_MODEL, tf), lambda i, j: (0, nf + j)),
                  pl.BlockSpec((tf, D_MODEL), lambda i, j: (j, 0))],
        out_specs=pl.BlockSpec((tm, D_MODEL), lambda i, j: (i, 0)),
        out_shape=jax.ShapeDtypeStruct((t, D_MODEL), f32),
        scratch_shapes=[pltpu.VMEM((tm, D_MODEL), bf16), pltpu.VMEM((tm, D_MODEL), f32)],
        compiler_params=_cparams(("parallel", "arbitrary")),
        name="ffn",
    )(x, g, w_in, w_in, w_out)


def _rope_tables(pos):
    half = DH_A // 2
    inv = ROPE_THETA ** (-jnp.arange(half, dtype=f32) / half)
    ang = pos.astype(f32)[:, None] * inv[None, :]
    cos, sin = jnp.cos(ang), jnp.sin(ang)
    cos_h = jnp.concatenate([cos, cos], axis=-1)
    sin_h = jnp.concatenate([-sin, sin], axis=-1)
    reps = W_A // DH_A
    return jnp.tile(cos_h, (1, reps)), jnp.tile(sin_h, (1, reps))


def _block_diag(w):
    nb, a, _ = w.shape
    eye = jnp.eye(nb, dtype=w.dtype)
    return (eye[:, None, :, None] * w[:, :, None, :]).reshape(nb * a, nb * a)


def _layer_params(l, P):
    w_in = P['w_in'][l]
    split_dt = 8 * 512 + XBC_D
    lp = {
        'norm_mix': P['norm_mix'][l][None],
        'w_main': jnp.concatenate([w_in[:, split_dt + H_D:], w_in[:, :split_dt]], axis=1).astype(bf16),
        'w_dt': jnp.pad(w_in[:, split_dt:split_dt + H_D], ((0, 0), (0, LANES - H_D))).astype(bf16),
        'q_gain': jnp.tile(P['q_norm'][l], 2)[None],
        'k_gain': jnp.tile(P['k_norm'][l], 2)[None],
        'lambda_qk': P['lambda_qk'][l],
        'subln': P['subln'][l].reshape(1, W_A),
        'lru_cw': P['lru_conv_w'][l], 'lru_cb': P['lru_conv_b'][l][None],
        'lru_wa': _block_diag(P['lru_wa'][l]).astype(bf16), 'lru_ba': P['lru_ba'][l][None],
        'lru_wx': _block_diag(P['lru_wx'][l]).astype(bf16), 'lru_bx': P['lru_bx'][l][None],
        'lru_lambda': P['lru_lambda'][l][None],
        'sgu_norm': P['sgu_norm'][l][None],
        'sgu_w': P['sgu_w'][l], 'sgu_b': P['sgu_b'][l],
        'ssd_cw': P['ssd_conv_w'][l], 'ssd_cb': P['ssd_conv_b'][l][None],
        'ssd_dtb': jnp.pad(P['ssd_dt_bias'][l], (0, LANES - H_D))[None],
        'ssd_alog': jnp.pad(P['ssd_a_log'][l], (0, LANES - H_D))[None],
        'ssd_d': jnp.repeat(P['ssd_d'][l], P_D)[None],
        'ssd_norm': P['ssd_norm'][l][None],
        'w_branch': P['w_branch'][l].astype(bf16), 'w_o': P['w_o'][l].astype(bf16),
        'norm_cross': P['norm_cross'][l][None], 'w_xq': P['w_xq'][l].astype(bf16),
        'xq_norm': P['xq_norm'][l][None], 'w_xo': P['w_xo'][l].astype(bf16),
        'norm_ffn': P['norm_ffn'][l][None], 'w_ffn_in': P['w_ffn_in'][l].astype(bf16),
        'w_ffn_out': P['w_ffn_out'][l].astype(bf16),
    }
    return lp


def _sgu_weights(lp, seq):
    w, bias = lp['sgu_w'], lp['sgu_b']
    if seq >= CHUNK_C:
        return w, bias.T
    reps = CHUNK_C // seq
    eye = jnp.eye(reps, dtype=w.dtype)
    wk = (eye[None, :, None, :, None] * w[:, None, :seq, None, :seq]).reshape(G_C, CHUNK_C, CHUNK_C)
    return wk, jnp.tile(bias[:, :seq], (1, reps)).T


def _head_bd():
    i = jnp.arange(LANES) // DH_A
    return jnp.where(i[:, None] == i[None, :], 1.0 / DH_A, 0.0).astype(bf16)


def _layer(l, lp, x, b, seq, rope_tab, lru_h0, lru_buf, ssm_h0, ssm_buf, mem_k, mem_v, mem_tall, paged=None):
    lam_init = 0.8 - 0.6 * math.exp(-0.3 * l)
    proj, dt_raw = _in_proj(x, lp['norm_mix'], lp['w_main'], lp['w_dt'])
    cos, sin = rope_tab
    act = f32 if paged is not None else bf16
    q, k, kb, vb = _qk_prep(proj, cos, sin, lp['q_gain'], lp['k_gain'], _head_bd(), act)
    if paged is None:
        y_a = _attn_prompt(lp['lambda_qk'], q, kb, vb, lp['subln'], lam_init, b, seq)
    else:
        page_table, cache_kt, cache_v2 = paged
        y_a = _attn_sample(page_table, lp['lambda_qk'], q, kb, vb, lp['subln'], cache_kt, cache_v2, l, lam_init, b, seq)
    y_b, lru_h = _lru(proj, lru_buf, lru_h0, lp['lru_cw'], lp['lru_cb'], lp['lru_wa'], lp['lru_ba'], lp['lru_wx'],
                      lp['lru_bx'], lp['lru_lambda'], b, seq)
    sgu_w, sgu_b = _sgu_weights(lp, seq)
    y_c, vc = _cmix(proj, lp['sgu_norm'], sgu_w, sgu_b)
    y_d, ssm_h = _ssd(proj, dt_raw, ssm_buf, ssm_h0, lp['ssd_cw'], lp['ssd_cb'], lp['ssd_dtb'], lp['ssd_alog'],
                      lp['ssd_d'], lp['ssd_norm'], b, seq)
    x = _merge(x, y_a, y_b, y_c, y_d, proj, lp['w_branch'], lp['w_o'])
    x = _cross(x, lp['norm_cross'], lp['w_xq'], lp['xq_norm'], mem_k, mem_v, lp['w_xo'], seq, mem_tall)
    x = _ffn(x, lp['norm_ffn'], lp['w_ffn_in'], lp['w_ffn_out'])
    p3 = proj.reshape(b, seq, N_MAIN)
    v = p3[:, :, COL_V * 512:(COL_V + 1) * 512]
    new_lru_buf = jnp.concatenate([lru_buf, p3[:, :, COL_XB * 512:(COL_XB + 1) * 512]], axis=1)[:, -(CONV_W - 1):]
    new_ssm_buf = jnp.concatenate([ssm_buf, p3[:, :, COL_XBC * 1024:(COL_XBC + 1) * 1024]], axis=1)[:, -(CONV_W - 1):]
    return x, (k, v, lru_h, new_lru_buf, ssm_h, new_ssm_buf, vc)


def kernel(x_prompt, x_sample, mem_prompt, cache_attn_k, cache_attn_v, page_table, cache_mem_k, cache_mem_v, state_lru, state_lru_conv, state_ssm, state_ssm_conv, norm_mix, w_in, q_norm, k_norm, lambda_qk, subln, lru_conv_w, lru_conv_b, lru_wa, lru_ba, lru_wx, lru_bx, lru_lambda, sgu_norm, sgu_w, sgu_b, ssd_conv_w, ssd_conv_b, ssd_dt_bias, ssd_a_log, ssd_d, ssd_norm, w_branch, w_o, norm_cross, w_xq, w_xk, w_xv, xq_norm, xk_norm, w_xo, norm_ffn, w_ffn_in, w_ffn_out):
    P = dict(norm_mix=norm_mix, w_in=w_in, q_norm=q_norm, k_norm=k_norm, lambda_qk=lambda_qk, subln=subln,
             lru_conv_w=lru_conv_w, lru_conv_b=lru_conv_b, lru_wa=lru_wa, lru_ba=lru_ba, lru_wx=lru_wx, lru_bx=lru_bx,
             lru_lambda=lru_lambda, sgu_norm=sgu_norm, sgu_w=sgu_w, sgu_b=sgu_b, ssd_conv_w=ssd_conv_w,
             ssd_conv_b=ssd_conv_b, ssd_dt_bias=ssd_dt_bias, ssd_a_log=ssd_a_log, ssd_d=ssd_d, ssd_norm=ssd_norm,
             w_branch=w_branch, w_o=w_o, norm_cross=norm_cross, w_xq=w_xq, xq_norm=xq_norm, w_xo=w_xo,
             norm_ffn=norm_ffn, w_ffn_in=w_ffn_in, w_ffn_out=w_ffn_out)
    bp, lp_, _ = x_prompt.shape
    bs, ls, _ = x_sample.shape
    n_pages = page_table.shape[1]
    past_len = n_pages * PAGE_SIZE
    n_phys = cache_attn_k.shape[1]

    rope_p = _rope_tables(jnp.arange(lp_))
    cos_s, sin_s = _rope_tables(past_len + jnp.arange(ls))
    rope_s = (jnp.tile(cos_s, (bs, 1)), jnp.tile(sin_s, (bs, 1)))
    cache_kt = jnp.transpose(cache_attn_k, (0, 1, 3, 4, 2))
    cache_v2 = cache_attn_v.reshape(DEPTH, n_phys, PAGE_SIZE * H_A, DV_A)
    mem_k_s = cache_mem_k.reshape(DEPTH, bs, N_MEM * XH, XDH)
    mem_v_s = cache_mem_v.reshape(DEPTH, bs, N_MEM * XH, XDH)

    pmk, pmv = _memkv(mem_prompt.reshape(bp * N_MEM, D_MODEL), w_xk.astype(bf16), w_xv.astype(bf16), xk_norm[:, None, :])

    yp = x_prompt.reshape(bp * lp_, D_MODEL)
    ys = x_sample.reshape(bs * ls, D_MODEL)
    zeros = lambda *s: jnp.zeros(s, f32)
    p_states, s_states = [], []
    for l in range(DEPTH):
        lp = _layer_params(l, P)
        yp, sp = _layer(l, lp, yp, bp, lp_, rope_p, zeros(bp, 1, W_B), zeros(bp, CONV_W - 1, W_B),
                        zeros(bp, H_D * P_D, N_D), zeros(bp, CONV_W - 1, XBC_D),
                        pmk[l].reshape(bp, N_MEM, XW), pmv[l].reshape(bp, N_MEM, XW), False)
        p_states.append(sp)
        ys, ss = _layer(l, lp, ys, bs, ls, rope_s, state_lru[l][:, None, :], state_lru_conv[l],
                        state_ssm[l].reshape(bs, H_D * P_D, N_D), state_ssm_conv[l],
                        mem_k_s[l], mem_v_s[l], True, paged=(page_table, cache_kt, cache_v2))
        s_states.append(ss)

    def stack(states, idx, shape):
        return jnp.stack([s[idx] for s in states]).reshape((DEPTH,) + shape)

    return (
        yp.reshape(bp, lp_, D_MODEL), ys.reshape(bs, ls, D_MODEL),
        stack(p_states, 0, (bp, lp_, 2 * H_A, DH_A)), stack(p_states, 1, (bp, lp_, H_A, DV_A)),
        pmk.reshape(DEPTH, bp, N_MEM, XH, XDH), pmv.reshape(DEPTH, bp, N_MEM, XH, XDH),
        stack(p_states, 2, (bp, W_B)), stack(p_states, 3, (bp, CONV_W - 1, W_B)),
        stack(p_states, 4, (bp, H_D, P_D, N_D)), stack(p_states, 5, (bp, CONV_W - 1, XBC_D)),
        stack(s_states, 0, (bs, ls, 2 * H_A, DH_A)), stack(s_states, 1, (bs, ls, H_A, DV_A)),
        stack(s_states, 2, (bs, W_B)), stack(s_states, 3, (bs, CONV_W - 1, W_B)),
        stack(s_states, 4, (bs, H_D, P_D, N_D)), stack(s_states, 5, (bs, CONV_W - 1, XBC_D)),
        stack(s_states, 6, (bs, ls, W_C)),
    )
```

```python
import functools
import math

import jax
import jax.numpy as jnp
from jax import lax
from jax.experimental import pallas as pl
from jax.experimental.pallas import tpu as pltpu

f32 = jnp.float32
bf16 = jnp.bfloat16

D_MODEL = 1024
DEPTH = 4
PAGE_SIZE = 128
H_A = 4
DH_A = 64
DV_A = 128
W_A = 512
ROPE_THETA = 10000.0
W_B = 512
NB_B = 8
LRU_C = 8.0
CONV_W = 4
W_C = 512
G_C = 4
CHUNK_C = 128
H_D = 8
P_D = 64
W_D = 512
G_D = 2
N_D = 128
SSD_CHUNK = 128
XBC_D = 1024
N_MEM = 256
XH = 4
XDH = 128
XW = 512
D_FF = 2816
EPS = 1e-6

LANES = 128
SUBLANES = 8
VMEM_LIMIT = 48 * 1024 * 1024

N_BRANCH = 4
N_MAIN = 5120
COL_Q, COL_K, COL_V, COL_XB, COL_GB, COL_UC, COL_VC, COL_ZD = range(8)
COL_XBC = 4


def _cparams(sem):
    return pltpu.CompilerParams(dimension_semantics=sem, vmem_limit_bytes=VMEM_LIMIT)


def _dot(a, b):
    return jnp.dot(a, b, preferred_element_type=f32)


def _dot_nt(a, b):
    return lax.dot_general(a, b, (((1,), (1,)), ((), ())), preferred_element_type=f32)


def _dot_tn(a, b):
    return lax.dot_general(a, b, (((0,), (0,)), ((), ())), preferred_element_type=f32)


def _rms(x, g):
    ms = jnp.mean(x * x, axis=-1, keepdims=True)
    return x * lax.rsqrt(ms + EPS) * g


def _softplus(x):
    return jnp.maximum(x, 0.0) + jnp.log1p(jnp.exp(-jnp.abs(x)))


def _in_proj_body(x_ref, g_ref, w_ref, wdt_ref, proj_ref, dt_ref, hn_ref):
    j = pl.program_id(1)

    @pl.when(j == 0)
    def _():
        hn = _rms(x_ref[...], g_ref[...]).astype(bf16)
        hn_ref[...] = hn
        dt_ref[...] = _dot(hn, wdt_ref[...])

    @pl.when((j <= COL_XB) | (j > COL_ZD))
    def _():
        proj_ref[...] = _dot(hn_ref[...], w_ref[...])

    @pl.when((j >= COL_GB) & (j <= COL_VC))
    def _():
        proj_ref[...] = jax.nn.gelu(_dot(hn_ref[...], w_ref[...]))

    @pl.when(j == COL_ZD)
    def _():
        acc = _dot(hn_ref[...], w_ref[...])
        proj_ref[...] = acc * jax.nn.sigmoid(acc)


def _in_proj(x, g, w_main, w_dt):
    t = x.shape[0]
    tm = min(t, 1024)
    tn = 512
    return pl.pallas_call(
        _in_proj_body,
        grid=(t // tm, N_MAIN // tn),
        in_specs=[
            pl.BlockSpec((tm, D_MODEL), lambda i, j: (i, 0)),
            pl.BlockSpec((1, D_MODEL), lambda i, j: (0, 0)),
            pl.BlockSpec((D_MODEL, tn), lambda i, j: (0, j)),
            pl.BlockSpec((D_MODEL, LANES), lambda i, j: (0, 0)),
        ],
        out_specs=[
            pl.BlockSpec((tm, tn), lambda i, j: (i, j)),
            pl.BlockSpec((tm, LANES), lambda i, j: (i, 0)),
        ],
        out_shape=[jax.ShapeDtypeStruct((t, N_MAIN), f32), jax.ShapeDtypeStruct((t, LANES), f32)],
        scratch_shapes=[pltpu.VMEM((tm, D_MODEL), bf16)],
        compiler_params=_cparams(("parallel", "arbitrary")),
        name="in_proj",
    )(x, g, w_main, w_dt)


def _gate_proj_body(x_ref, g_ref, w_ref, o_ref, hn_ref):
    @pl.when(pl.program_id(1) == 0)
    def _():
        hn_ref[...] = _rms(x_ref[...], g_ref[...]).astype(bf16)

    o_ref[...] = jax.nn.sigmoid(_dot(hn_ref[...], w_ref[...])).astype(o_ref.dtype)


def _gate_proj(x, g, w_gates):
    t = x.shape[0]
    tm = min(t, 1024)
    tn = 1024
    return pl.pallas_call(
        _gate_proj_body,
        grid=(t // tm, N_BRANCH * D_MODEL // tn),
        in_specs=[
            pl.BlockSpec((tm, D_MODEL), lambda i, j: (i, 0)),
            pl.BlockSpec((1, D_MODEL), lambda i, j: (0, 0)),
            pl.BlockSpec((D_MODEL, tn), lambda i, j: (0, j)),
        ],
        out_specs=pl.BlockSpec((tm, tn), lambda i, j: (i, j)),
        out_shape=jax.ShapeDtypeStruct((t, N_BRANCH * D_MODEL), bf16),
        scratch_shapes=[pltpu.VMEM((tm, D_MODEL), bf16)],
        compiler_params=_cparams(("parallel", "arbitrary")),
        name="gate_proj",
    )(x, g, w_gates)


def _head_norm_rope(x, gain, cos, sin, bd):
    sq = x * x
    hi = sq.astype(bf16)
    lo = (sq - hi.astype(f32)).astype(bf16)
    ms = _dot(hi, bd) + _dot(lo, bd)
    y = x * lax.rsqrt(ms + EPS) * gain
    lane = lax.broadcasted_iota(jnp.int32, y.shape, 1)
    first_half = (lane % DH_A) < (DH_A // 2)
    swapped = jnp.where(first_half, pltpu.roll(y, LANES - DH_A // 2, 1), pltpu.roll(y, DH_A // 2, 1))
    return y * cos + swapped * sin


def _qk_prep_body(q_ref, k_ref, v_ref, cos_ref, sin_ref, qg_ref, kg_ref, bd_ref, qo_ref, ko_ref, kb_ref, vb_ref, *,
                  q_scale):
    bd = bd_ref[...]
    for c in range(W_A // LANES):
        sl = slice(c * LANES, (c + 1) * LANES)
        cos = cos_ref[:, sl]
        sin = sin_ref[:, sl]
        q = _head_norm_rope(q_ref[:, sl], qg_ref[...], cos, sin, bd)
        k = _head_norm_rope(k_ref[:, sl], kg_ref[...], cos, sin, bd)
        qo_ref[:, sl] = (q * q_scale).astype(qo_ref.dtype)
        ko_ref[:, sl] = k
        kb_ref[:, sl] = k.astype(kb_ref.dtype)
    vb_ref[...] = v_ref[...].astype(vb_ref.dtype)


def _qk_prep(proj, cos, sin, qg, kg, bd, act_dtype, q_scale):
    t = proj.shape[0]
    p = cos.shape[0]
    tm = min(t, p, 512)
    npos = p // tm
    row = lambda c: pl.BlockSpec((tm, 512), lambda i, c=c: (i, c))
    tab = pl.BlockSpec((tm, 512), lambda i: (i % npos, 0))
    small = lambda shape: pl.BlockSpec(shape, lambda i: (0, 0))
    return pl.pallas_call(
        functools.partial(_qk_prep_body, q_scale=q_scale),
        grid=(t // tm,),
        in_specs=[row(COL_Q), row(COL_K), row(COL_V), tab, tab, small((1, LANES)), small((1, LANES)),
                  small((LANES, LANES))],
        out_specs=[row(0)] * 4,
        out_shape=[jax.ShapeDtypeStruct((t, 512), act_dtype), jax.ShapeDtypeStruct((t, 512), f32),
                   jax.ShapeDtypeStruct((t, 512), act_dtype), jax.ShapeDtypeStruct((t, 512), act_dtype)],
        compiler_params=_cparams(("parallel",)),
        name="qk_prep",
    )(proj, proj, proj, cos, sin, qg, kg, bd)


def _lambda_value(lq, lam_init):
    a = jnp.sum(lq[0:1] * lq[1:2], axis=-1, keepdims=True)
    b = jnp.sum(lq[2:3] * lq[3:4], axis=-1, keepdims=True)
    return jnp.exp(a) - jnp.exp(b) + lam_init


def _stack_pair(qp):
    lane = lax.broadcasted_iota(jnp.int32, qp.shape, 1)
    zero = jnp.zeros_like(qp)
    return jnp.concatenate([jnp.where(lane < DH_A, qp, zero), jnp.where(lane >= DH_A, qp, zero)], axis=0)


def _diff_finish(acc, l, lam, sub, lam_init, tq):
    o = acc[:tq] / l[:tq] - lam * (acc[tq:] / l[tq:])
    return _rms(o, sub) * (1.0 - lam_init)


def _attn_prompt_body(lq_ref, q_ref, k_ref, v_ref, sub_ref, o_ref, *, lam_init, tq, tk):
    qi = pl.program_id(1)
    lam = _lambda_value(lq_ref[...], lam_init)
    per = tk // tq
    n_full = qi // per
    r = lax.broadcasted_iota(jnp.int32, (2 * tq, tk), 0)
    r = jnp.where(r >= tq, r - tq, r) + (qi % per) * tq
    c = lax.broadcasted_iota(jnp.int32, (2 * tq, tk), 1)
    diag = c <= r
    sls = [slice(h * LANES, (h + 1) * LANES) for h in range(H_A)]
    qqs = [_stack_pair(q_ref[:, sl]) for sl in sls]

    def step(kv, carry, masked):
        rows = pl.ds(pl.multiple_of(kv * tk, tk), tk)
        out = []
        for (m, l, acc), qq, sl in zip(carry, qqs, sls):
            s = _dot_nt(qq, k_ref[rows, sl])
            if masked:
                s = jnp.where(diag, s, -jnp.inf)
            m_new = jnp.maximum(m, jnp.max(s, axis=-1, keepdims=True))
            alpha = jnp.exp2(m - m_new)
            p = jnp.exp2(s - m_new)
            l = alpha * l + jnp.sum(p, axis=-1, keepdims=True)
            acc = alpha * acc + _dot(p.astype(bf16), v_ref[rows, sl])
            out.append((m_new, l, acc))
        return tuple(out)

    init = tuple((jnp.full((2 * tq, 1), -jnp.inf, f32), jnp.zeros((2 * tq, 1), f32), jnp.zeros((2 * tq, LANES), f32))
                 for _ in range(H_A))
    carry = lax.fori_loop(0, n_full, functools.partial(step, masked=False), init)
    carry = step(n_full, carry, True)
    for (_, l, acc), sl in zip(carry, sls):
        o_ref[:, sl] = _diff_finish(acc, l, lam, sub_ref[:, sl], lam_init, tq)


ATTN_TQ = 256
ATTN_TK = 512


def _attn_prompt(lq, q, k, v, sub, lam_init, b, seq):
    tq = min(seq, ATTN_TQ)
    tk = min(seq, ATTN_TK)
    nq = seq // tq
    return pl.pallas_call(
        functools.partial(_attn_prompt_body, lam_init=lam_init, tq=tq, tk=tk),
        grid=(b, nq),
        in_specs=[
            pl.BlockSpec((4, DH_A), lambda i, j: (0, 0)),
            pl.BlockSpec((tq, 512), lambda i, j: (i * nq + j, 0)),
            pl.BlockSpec((seq, 512), lambda i, j: (i, 0)),
            pl.BlockSpec((seq, 512), lambda i, j: (i, 0)),
            pl.BlockSpec((1, 512), lambda i, j: (0, 0)),
        ],
        out_specs=pl.BlockSpec((tq, 512), lambda i, j: (i * nq + j, 0)),
        out_shape=jax.ShapeDtypeStruct((b * seq, 512), f32),
        compiler_params=_cparams(("parallel", "arbitrary")),
        name="attn_prompt",
    )(lq, q, k, v, sub)


def _attn_sample_body(pt_ref, lq_ref, q_ref, kn_ref, vn_ref, sub_ref, *rest, lam_init, n_pages, ls):
    del pt_ref
    k_refs = rest[:n_pages]
    v_refs = rest[n_pages:2 * n_pages]
    o_ref = rest[2 * n_pages]
    lam = _lambda_value(lq_ref[...], lam_init)
    pad = jnp.zeros((PAGE_SIZE - ls, LANES), f32)
    r = lax.broadcasted_iota(jnp.int32, (2 * ls, PAGE_SIZE), 0)
    r = jnp.where(r >= ls, r - ls, r)
    c = lax.broadcasted_iota(jnp.int32, (2 * ls, PAGE_SIZE), 1)
    new_mask = c <= r
    for h in range(H_A):
        sl = slice(h * LANES, (h + 1) * LANES)
        qq = _stack_pair(q_ref[:, sl]).astype(bf16)
        kt = jnp.concatenate([k_refs[p][2 * h:2 * h + 2].reshape(LANES, PAGE_SIZE) for p in range(n_pages)], axis=1)
        s_past = _dot(qq, kt.astype(bf16))
        k_new = jnp.concatenate([kn_ref[:, sl], pad], axis=0).astype(bf16)
        s_new = jnp.where(new_mask, _dot_nt(qq, k_new), -jnp.inf)
        m = jnp.maximum(jnp.max(s_past, axis=-1, keepdims=True), jnp.max(s_new, axis=-1, keepdims=True))
        p_past = jnp.exp(s_past - m)
        p_new = jnp.exp(s_new - m)
        l = jnp.sum(p_past, axis=-1, keepdims=True) + jnp.sum(p_new, axis=-1, keepdims=True)
        v_past = jnp.concatenate([v_refs[p][pl.ds(h, PAGE_SIZE, stride=H_A), :] for p in range(n_pages)], axis=0)
        v_new = jnp.concatenate([vn_ref[:, sl], pad], axis=0).astype(bf16)
        acc = _dot(p_past.astype(bf16), v_past.astype(bf16)) + _dot(p_new.astype(bf16), v_new)
        o_ref[:, sl] = _diff_finish(acc, l, lam, sub_ref[:, sl], lam_init, ls)


def _attn_sample(page_table, lq, q, k_new, v_new, sub, cache_kt, cache_v2, layer, lam_init, b, ls):
    n_pages = page_table.shape[1]
    row = pl.BlockSpec((ls, 512), lambda i, pt: (i, 0))
    k_specs = [pl.BlockSpec((None, None, 2 * H_A, DH_A, PAGE_SIZE), lambda i, pt, p=p: (layer, pt[i, p], 0, 0, 0))
               for p in range(n_pages)]
    v_specs = [pl.BlockSpec((None, None, PAGE_SIZE * H_A, DV_A), lambda i, pt, p=p: (layer, pt[i, p], 0, 0))
               for p in range(n_pages)]
    grid_spec = pltpu.PrefetchScalarGridSpec(
        num_scalar_prefetch=1,
        grid=(b,),
        in_specs=[pl.BlockSpec((4, DH_A), lambda i, pt: (0, 0)), row, row, row,
                  pl.BlockSpec((1, 512), lambda i, pt: (0, 0))] + k_specs + v_specs,
        out_specs=row,
    )
    return pl.pallas_call(
        functools.partial(_attn_sample_body, lam_init=lam_init, n_pages=n_pages, ls=ls),
        grid_spec=grid_spec,
        out_shape=jax.ShapeDtypeStruct((b * ls, 512), f32),
        compiler_params=_cparams(("parallel",)),
        name="attn_sample",
    )(page_table, lq, q, k_new, v_new, sub, *([cache_kt] * n_pages), *([cache_v2] * n_pages))


def _conv_tile(x_ref, buf_ref, w_ref, b_ref, xx_ref, first, tt):
    @pl.when(first)
    def _():
        xx_ref[SUBLANES - (CONV_W - 1):SUBLANES, :] = buf_ref[...]

    @pl.when(jnp.logical_not(first))
    def _():
        xx_ref[SUBLANES - (CONV_W - 1):SUBLANES, :] = xx_ref[tt + SUBLANES - (CONV_W - 1):tt + SUBLANES, :]

    xx_ref[SUBLANES:SUBLANES + tt, :] = x_ref[...]
    y = b_ref[...] + w_ref[0:1, :] * xx_ref[SUBLANES - 3:SUBLANES - 3 + tt, :]
    for j in range(1, CONV_W):
        y = y + w_ref[j:j + 1, :] * xx_ref[SUBLANES - 3 + j:SUBLANES - 3 + j + tt, :]
    return y


def _lru_body(x_ref, gate_ref, buf_ref, h0_ref, cw_ref, cb_ref, wa_ref, ba_ref, wx_ref, bx_ref, lam_ref,
              y_ref, hf_ref, xx_ref, a_ref, u_ref, h_ref, *, n_seq, tt):
    ti = pl.program_id(1)
    xc = [_conv_tile(x_ref.at[pl.ds(s * tt, tt)], buf_ref.at[s], cw_ref, cb_ref, xx_ref.at[s], ti == 0, tt)
          for s in range(n_seq)]
    xc = xc[0] if n_seq == 1 else jnp.concatenate(xc, axis=0)
    xcb = xc.astype(bf16)
    r = jax.nn.sigmoid(_dot(xcb, wa_ref[...]) + ba_ref[...])
    i = jax.nn.sigmoid(_dot(xcb, wx_ref[...]) + bx_ref[...])
    log_a = (-LRU_C) * r * _softplus(-lam_ref[...])
    a_ref[...] = jnp.exp(log_a)
    th = jnp.tanh(log_a)
    u_ref[...] = jnp.sqrt(-2.0 * th / (1.0 - th)) * (i * xc)

    @pl.when(ti == 0)
    def _():
        h_ref[...] = h0_ref[...]

    row = lax.broadcasted_iota(jnp.int32, (SUBLANES, W_B), 0)

    def tile(start, h):
        rows = pl.ds(pl.multiple_of(start, SUBLANES), SUBLANES)
        a = a_ref[rows, :]
        u = u_ref[rows, :]
        for sh in (1, 2, 4):
            keep = row >= sh
            u = jnp.where(keep, a * pltpu.roll(u, sh, 0) + u, u)
            a = jnp.where(keep, a * pltpu.roll(a, sh, 0), a)
        hh = a * h + u
        y_ref[rows, :] = hh * gate_ref[rows, :]
        return hh[SUBLANES - 1:SUBLANES, :]

    for s in range(n_seq):
        if tt == SUBLANES:
            h = tile(s * tt, h_ref[s])
        else:
            h = lax.fori_loop(0, tt // SUBLANES, lambda k, h, s=s: tile(s * tt + k * SUBLANES, h), h_ref[s])
        h_ref[s] = h
        hf_ref[s] = h


def _lru(proj, buf, h0, cw, cb, wa, ba, wx, bx, lam, layer, b, seq):
    n_seq, tt = (1, min(seq, 512)) if seq >= LANES else (LANES // seq, seq)
    nt = seq // tt
    rows = n_seq * tt
    small = lambda shape: pl.BlockSpec(shape, lambda i, j: (0,) * len(shape))
    return pl.pallas_call(
        functools.partial(_lru_body, n_seq=n_seq, tt=tt),
        grid=(b // n_seq, nt),
        in_specs=[
            pl.BlockSpec((rows, 512), lambda i, j: (i * nt + j, COL_XB)),
            pl.BlockSpec((rows, 512), lambda i, j: (i * nt + j, COL_GB)),
            pl.BlockSpec((None, n_seq, CONV_W - 1, W_B), lambda i, j: (layer, i, 0, 0)),
            pl.BlockSpec((None, n_seq, 1, W_B), lambda i, j: (layer, i, 0, 0)),
            small((CONV_W, W_B)), small((1, W_B)), small((W_B, W_B)), small((1, W_B)), small((W_B, W_B)),
            small((1, W_B)), small((1, W_B)),
        ],
        out_specs=[
            pl.BlockSpec((rows, 512), lambda i, j: (i * nt + j, 0)),
            pl.BlockSpec((n_seq, 1, W_B), lambda i, j: (i, 0, 0)),
        ],
        out_shape=[jax.ShapeDtypeStruct((b * seq, W_B), f32), jax.ShapeDtypeStruct((b, 1, W_B), f32)],
        scratch_shapes=[pltpu.VMEM((n_seq, tt + SUBLANES, W_B), f32), pltpu.VMEM((rows, W_B), f32),
                        pltpu.VMEM((rows, W_B), f32), pltpu.VMEM((n_seq, 1, W_B), f32)],
        compiler_params=_cparams(("parallel", "arbitrary")),
        name="lru",
    )(proj, proj, buf, h0, cw, cb, wa, ba, wx, bx, lam)


def _cmix_body(u_ref, v_ref, g_ref, w_ref, b_ref, y_ref, vn_ref, *, n_chunks):
    vn = _rms(v_ref[...], g_ref[...])
    vn_ref[...] = vn
    r = lax.broadcasted_iota(jnp.int32, (CHUNK_C, CHUNK_C), 0)
    c = lax.broadcasted_iota(jnp.int32, (CHUNK_C, CHUNK_C), 1)
    for g in range(G_C):
        sl = slice(g * LANES, (g + 1) * LANES)
        wg = jnp.where(c <= r, w_ref[g], 0.0).astype(bf16)
        bias = b_ref[:, g:g + 1]
        for ch in range(n_chunks):
            rows = slice(ch * CHUNK_C, (ch + 1) * CHUNK_C)
            mixed = _dot(wg, vn[rows, sl].astype(bf16)) + bias
            y_ref[rows, sl] = u_ref[rows, sl] * mixed


def _cmix(proj, g, w, bias):
    t = proj.shape[0]
    tc = min(t, 512)
    return pl.pallas_call(
        functools.partial(_cmix_body, n_chunks=tc // CHUNK_C),
        grid=(t // tc,),
        in_specs=[
            pl.BlockSpec((tc, 512), lambda i: (i, COL_UC)),
            pl.BlockSpec((tc, 512), lambda i: (i, COL_VC)),
            pl.BlockSpec((1, W_C), lambda i: (0, 0)),
            pl.BlockSpec((G_C, CHUNK_C, CHUNK_C), lambda i: (0, 0, 0)),
            pl.BlockSpec((CHUNK_C, G_C), lambda i: (0, 0)),
        ],
        out_specs=[pl.BlockSpec((tc, 512), lambda i: (i, 0))] * 2,
        out_shape=[jax.ShapeDtypeStruct((t, W_C), f32)] * 2,
        compiler_params=_cparams(("parallel",)),
        name="cmix",
    )(proj, proj, g, w, bias)


def _ssd_body(xbc_ref, z_ref, dt_ref, buf_ref, h0_ref, cw_ref, cb_ref, dtb_ref, alog_ref, d_ref, g_ref,
              y_ref, hf_ref, xx_ref, st_ref, *, rows_in):
    ci = pl.program_id(1)
    q = SSD_CHUNK

    @pl.when(ci == 0)
    def _():
        st_ref[...] = h0_ref[...]

    xbc = _conv_tile(xbc_ref, buf_ref, cw_ref, cb_ref, xx_ref, ci == 0, rows_in)
    xbc = xbc * jax.nn.sigmoid(xbc)
    dtx = dt_ref[...] + dtb_ref[...]
    dt = jnp.maximum(dtx, 0.0) + jnp.log(1.0 + jnp.exp(-jnp.abs(dtx)))
    z = z_ref[...]
    if rows_in < q:
        padr = lambda a: jnp.concatenate([a, jnp.zeros((q - rows_in, a.shape[1]), f32)], axis=0)
        xbc, dt, z = padr(xbc), padr(dt), padr(z)
    xs = xbc[:, :W_D]
    a_neg = -jnp.exp(alog_ref[...])
    ri = lax.broadcasted_iota(jnp.int32, (q, q), 0)
    cj = lax.broadcasted_iota(jnp.int32, (q, q), 1)
    causal = cj <= ri
    tril = jnp.where(causal, 1.0, 0.0).astype(f32)
    cum = jnp.dot(tril, dt * a_neg, preferred_element_type=f32, precision=lax.Precision.HIGHEST)
    cum_t = cum.T
    lane = lax.broadcasted_iota(jnp.int32, (q, LANES), 1)
    low = lane < P_D
    rlow = lax.broadcasted_iota(jnp.int32, (LANES, N_D), 0) < P_D
    y_parts = []
    for g in range(G_D):
        bm = xbc[:, W_D + g * N_D:W_D + (g + 1) * N_D].astype(bf16)
        cm = xbc[:, W_D + (G_D + g) * N_D:W_D + (G_D + g + 1) * N_D].astype(bf16)
        scores = _dot_nt(cm, bm)
        for k in range(H_D // G_D // 2):
            h0 = g * (H_D // G_D) + 2 * k
            sl = slice(h0 * P_D, (h0 + 2) * P_D)
            col = lambda a, h: a[:, h:h + 1]
            pair = lambda a: jnp.where(low, col(a, h0), col(a, h0 + 1))
            xdt = xs[:, sl] * pair(dt)
            xdt_b = xdt.astype(bf16)
            yd = []
            for h in (h0, h0 + 1):
                seg = col(cum, h) - cum_t[h:h + 1, :]
                m = scores * jnp.exp(jnp.where(causal, seg, -jnp.inf))
                yd.append(_dot(m.astype(bf16), xdt_b))
            st = st_ref[sl, :]
            y_off = _dot_nt(cm, st.astype(bf16)) * jnp.exp(pair(cum))
            y_parts.append(jnp.where(low, yd[0], yd[1]) + y_off + d_ref[:, sl] * xs[:, sl])
            last = cum[q - 1:q, :]
            xw = xdt * jnp.exp(pair(last - cum))
            dec = jnp.exp(jnp.where(rlow, last[:, h0:h0 + 1], last[:, h0 + 1:h0 + 2]))
            st_ref[sl, :] = dec * st + _dot_tn(xw.astype(bf16), bm)
    y = jnp.concatenate(y_parts, axis=1) * z
    y_ref[...] = _rms(y, g_ref[...])[:rows_in]
    hf_ref[...] = st_ref[...]


def _ssd(proj, dt_raw, buf, h0, cw, cb, dtb, alog, dvec, g, layer, b, seq):
    rows = min(seq, SSD_CHUNK)
    nc = seq // rows
    small = lambda shape: pl.BlockSpec(shape, lambda i, j: (0,) * len(shape))
    return pl.pallas_call(
        functools.partial(_ssd_body, rows_in=rows),
        grid=(b, nc),
        in_specs=[
            pl.BlockSpec((rows, XBC_D), lambda i, j: (i * nc + j, COL_XBC)),
            pl.BlockSpec((rows, 512), lambda i, j: (i * nc + j, COL_ZD)),
            pl.BlockSpec((rows, LANES), lambda i, j: (i * nc + j, 0)),
            pl.BlockSpec((None, None, CONV_W - 1, XBC_D), lambda i, j: (layer, i, 0, 0)),
            pl.BlockSpec((None, None, H_D * P_D, N_D), lambda i, j: (layer, i, 0, 0)),
            small((CONV_W, XBC_D)), small((1, XBC_D)), small((1, LANES)), small((1, LANES)), small((1, W_D)),
            small((1, W_D)),
        ],
        out_specs=[
            pl.BlockSpec((rows, W_D), lambda i, j: (i * nc + j, 0)),
            pl.BlockSpec((None, H_D * P_D, N_D), lambda i, j: (i, 0, 0)),
        ],
        out_shape=[jax.ShapeDtypeStruct((b * seq, W_D), f32), jax.ShapeDtypeStruct((b, H_D * P_D, N_D), f32)],
        scratch_shapes=[pltpu.VMEM((rows + SUBLANES, XBC_D), f32), pltpu.VMEM((H_D * P_D, N_D), f32)],
        compiler_params=_cparams(("parallel", "arbitrary")),
        name="ssd",
    )(proj, proj, dt_raw, buf, h0, cw, cb, dtb, alog, dvec, g)


def _merge_body(x_ref, ya_ref, yb_ref, yc_ref, yd_ref, gates_ref, wb_ref, wo_ref, o_ref):
    m = None
    for k, y_ref in enumerate((ya_ref, yb_ref, yc_ref, yd_ref)):
        gate = gates_ref[:, k * D_MODEL:(k + 1) * D_MODEL].astype(f32)
        t = gate * _dot(y_ref[...].astype(bf16), wb_ref[k])
        m = t if m is None else m + t
    o_ref[...] = x_ref[...] + _dot(m.astype(bf16), wo_ref[...])


def _merge(x, ya, yb, yc, yd, gates, wb, wo):
    t = x.shape[0]
    tm = min(t, 512)
    row = lambda w, c=0: pl.BlockSpec((tm, w), lambda i, c=c: (i, c))
    return pl.pallas_call(
        _merge_body,
        grid=(t // tm,),
        in_specs=[row(D_MODEL), row(512), row(512), row(512), row(512),
                  pl.BlockSpec((tm, 4 * D_MODEL), lambda i: (i, 0)),
                  pl.BlockSpec((4, 512, D_MODEL), lambda i: (0, 0, 0)),
                  pl.BlockSpec((D_MODEL, D_MODEL), lambda i: (0, 0))],
        out_specs=row(D_MODEL),
        out_shape=jax.ShapeDtypeStruct((t, D_MODEL), f32),
        compiler_params=_cparams(("parallel",)),
        name="merge",
    )(x, ya, yb, yc, yd, gates, wb, wo)


def _memkv_body(m_ref, wk_ref, wv_ref, g_ref, k_ref, v_ref):
    mb = m_ref[...].astype(bf16)
    kk = _dot(mb, wk_ref[...])
    for h in range(XH):
        sl = slice(h * XDH, (h + 1) * XDH)
        k_ref[:, sl] = _rms(kk[:, sl], g_ref[...])
    v_ref[...] = _dot(mb, wv_ref[...])


def _memkv(mem, wk, wv, g):
    r = mem.shape[0]
    tm = min(r, 512)
    return pl.pallas_call(
        _memkv_body,
        grid=(DEPTH, r // tm),
        in_specs=[pl.BlockSpec((tm, D_MODEL), lambda l, i: (i, 0)),
                  pl.BlockSpec((None, D_MODEL, XW), lambda l, i: (l, 0, 0)),
                  pl.BlockSpec((None, D_MODEL, XW), lambda l, i: (l, 0, 0)),
                  pl.BlockSpec((None, 1, XDH), lambda l, i: (l, 0, 0))],
        out_specs=[pl.BlockSpec((None, tm, XW), lambda l, i: (l, i, 0))] * 2,
        out_shape=[jax.ShapeDtypeStruct((DEPTH, r, XW), f32)] * 2,
        compiler_params=_cparams(("parallel", "parallel")),
        name="memkv",
    )(mem, wk, wv, g)


def _cross_body(x_ref, g_ref, wq_ref, qg_ref, k_ref, v_ref, wo_ref, o_ref, *, n_seq, rows, tall):
    x = x_ref[...]
    qx = _dot(_rms(x, g_ref[...]).astype(bf16), wq_ref[...])
    qn = [_rms(qx[:, h * XDH:(h + 1) * XDH], qg_ref[...]) for h in range(XH)]

    def softmax(sc):
        p = jnp.exp(sc - jnp.max(sc, axis=-1, keepdims=True))
        return (p / jnp.sum(p, axis=-1, keepdims=True)).astype(bf16)

    if tall:
        r = lax.broadcasted_iota(jnp.int32, (XH * rows, N_MEM * XH), 0)
        c = lax.broadcasted_iota(jnp.int32, (XH * rows, N_MEM * XH), 1)
        same_head = (c % XH) == (r // rows)
        outs = []
        for s in range(n_seq):
            qs = jnp.concatenate([q[s * rows:(s + 1) * rows] for q in qn], axis=0)
            sc = _dot_nt(qs.astype(bf16), k_ref[s].astype(bf16)) * (XDH ** -0.5)
            o = _dot(softmax(jnp.where(same_head, sc, -jnp.inf)), v_ref[s].astype(bf16))
            outs.append(jnp.concatenate([o[h * rows:(h + 1) * rows] for h in range(XH)], axis=1))
        ox = jnp.concatenate(outs, axis=0)
    else:
        heads = []
        for h in range(XH):
            sl = slice(h * XDH, (h + 1) * XDH)
            outs = []
            for s in range(n_seq):
                sc = _dot_nt(qn[h][s * rows:(s + 1) * rows].astype(bf16), k_ref[s, :, sl].astype(bf16)) * (XDH ** -0.5)
                outs.append(_dot(softmax(sc), v_ref[s, :, sl].astype(bf16)))
            heads.append(outs[0] if n_seq == 1 else jnp.concatenate(outs, axis=0))
        ox = jnp.concatenate(heads, axis=1)
    o_ref[...] = x + _dot(ox.astype(bf16), wo_ref[...])


def _cross(x, g, wq, qg, mem_k, mem_v, wo, layer, seq, tall):
    t = x.shape[0]
    if seq >= 512:
        n_seq, rows = 1, 512
        per = seq // rows
        mem_idx = lambda i: (layer, i // per, 0, 0)
    else:
        n_seq, rows = 8, seq
        mem_idx = lambda i: (layer, i, 0, 0)
    tm = n_seq * rows
    small = lambda shape: pl.BlockSpec(shape, lambda i: (0,) * len(shape))
    mem_block = (None, n_seq) + tuple(mem_k.shape[2:])
    return pl.pallas_call(
        functools.partial(_cross_body, n_seq=n_seq, rows=rows, tall=tall),
        grid=(t // tm,),
        in_specs=[pl.BlockSpec((tm, D_MODEL), lambda i: (i, 0)), small((1, D_MODEL)), small((D_MODEL, XW)),
                  small((1, XDH)), pl.BlockSpec(mem_block, mem_idx), pl.BlockSpec(mem_block, mem_idx),
                  small((XW, D_MODEL))],
        out_specs=pl.BlockSpec((tm, D_MODEL), lambda i: (i, 0)),
        out_shape=jax.ShapeDtypeStruct((t, D_MODEL), f32),
        compiler_params=_cparams(("parallel",)),
        name="cross",
    )(x, g, wq, qg, mem_k, mem_v, wo)


def _ffn_body(x_ref, g_ref, wg_ref, wu_ref, wo_ref, o_ref, hn_ref, acc_ref):
    j = pl.program_id(1)

    @pl.when(j == 0)
    def _():
        hn_ref[...] = _rms(x_ref[...], g_ref[...]).astype(bf16)
        acc_ref[...] = jnp.zeros_like(acc_ref)

    hn = hn_ref[...]
    gate = _dot(hn, wg_ref[...])
    up = _dot(hn, wu_ref[...])
    act = (gate * jax.nn.sigmoid(gate) * up).astype(bf16)
    acc_ref[...] += _dot(act, wo_ref[...])

    @pl.when(j == pl.num_programs(1) - 1)
    def _():
        o_ref[...] = x_ref[...] + acc_ref[...]


def _ffn(x, g, w_in, w_out):
    t = x.shape[0]
    tm = min(t, 1024)
    tf = 256
    nf = D_FF // tf
    return pl.pallas_call(
        _ffn_body,
        grid=(t // tm, nf),
        in_specs=[pl.BlockSpec((tm, D_MODEL), lambda i, j: (i, 0)),
                  pl.BlockSpec((1, D_MODEL), lambda i, j: (0, 0)),
                  pl.BlockSpec((D_MODEL, tf), lambda i, j: (0, j)),
                  pl.BlockSpec((D_MODEL, tf), lambda i, j: (0, nf + j)),
                  pl.BlockSpec((tf, D_MODEL), lambda i, j: (j, 0))],
        out_specs=pl.BlockSpec((tm, D_MODEL), lambda i, j: (i, 0)),
        out_shape=jax.ShapeDtypeStruct((t, D_MODEL), f32),
        scratch_shapes=[pltpu.VMEM((tm, D_MODEL), bf16), pltpu.VMEM((tm, D_MODEL), f32)],
        compiler_params=_cparams(("parallel", "arbitrary")),
        name="ffn",
    )(x, g, w_in, w_in, w_out)


def _rope_tables(pos):
    half = DH_A // 2
    inv = ROPE_THETA ** (-jnp.arange(half, dtype=f32) / half)
    ang = pos.astype(f32)[:, None] * inv[None, :]
    cos, sin = jnp.cos(ang), jnp.sin(ang)
    cos_h = jnp.concatenate([cos, cos], axis=-1)
    sin_h = jnp.concatenate([-sin, sin], axis=-1)
    reps = W_A // DH_A
    return jnp.tile(cos_h, (1, reps)), jnp.tile(sin_h, (1, reps))


def _block_diag(w):
    nb, a, _ = w.shape
    eye = jnp.eye(nb, dtype=w.dtype)
    return (eye[:, None, :, None] * w[:, :, None, :]).reshape(nb * a, nb * a)


def _layer_params(l, P):
    w_in = P['w_in'][l]
    split_dt = 8 * 512 + XBC_D
    lp = {
        'norm_mix': P['norm_mix'][l][None],
        'w_main': w_in[:, :split_dt].astype(bf16),
        'w_gates': w_in[:, split_dt + H_D:].astype(bf16),
        'w_dt': jnp.pad(w_in[:, split_dt:split_dt + H_D], ((0, 0), (0, LANES - H_D))).astype(bf16),
        'q_gain': jnp.tile(P['q_norm'][l], 2)[None],
        'k_gain': jnp.tile(P['k_norm'][l], 2)[None],
        'lambda_qk': P['lambda_qk'][l],
        'subln': P['subln'][l].reshape(1, W_A),
        'lru_cw': P['lru_conv_w'][l], 'lru_cb': P['lru_conv_b'][l][None],
        'lru_wa': _block_diag(P['lru_wa'][l]).astype(bf16), 'lru_ba': P['lru_ba'][l][None],
        'lru_wx': _block_diag(P['lru_wx'][l]).astype(bf16), 'lru_bx': P['lru_bx'][l][None],
        'lru_lambda': P['lru_lambda'][l][None],
        'sgu_norm': P['sgu_norm'][l][None],
        'sgu_w': P['sgu_w'][l], 'sgu_b': P['sgu_b'][l],
        'ssd_cw': P['ssd_conv_w'][l], 'ssd_cb': P['ssd_conv_b'][l][None],
        'ssd_dtb': jnp.pad(P['ssd_dt_bias'][l], (0, LANES - H_D))[None],
        'ssd_alog': jnp.pad(P['ssd_a_log'][l], (0, LANES - H_D))[None],
        'ssd_d': jnp.repeat(P['ssd_d'][l], P_D)[None],
        'ssd_norm': P['ssd_norm'][l][None],
        'w_branch': P['w_branch'][l].astype(bf16), 'w_o': P['w_o'][l].astype(bf16),
        'norm_cross': P['norm_cross'][l][None], 'w_xq': P['w_xq'][l].astype(bf16),
        'xq_norm': P['xq_norm'][l][None], 'w_xo': P['w_xo'][l].astype(bf16),
        'norm_ffn': P['norm_ffn'][l][None], 'w_ffn_in': P['w_ffn_in'][l].astype(bf16),
        'w_ffn_out': P['w_ffn_out'][l].astype(bf16),
    }
    return lp


def _sgu_weights(lp, seq):
    w, bias = lp['sgu_w'], lp['sgu_b']
    if seq >= CHUNK_C:
        return w, bias.T
    reps = CHUNK_C // seq
    eye = jnp.eye(reps, dtype=w.dtype)
    wk = (eye[None, :, None, :, None] * w[:, None, :seq, None, :seq]).reshape(G_C, CHUNK_C, CHUNK_C)
    return wk, jnp.tile(bias[:, :seq], (1, reps)).T


def _head_bd():
    i = jnp.arange(LANES) // DH_A
    return jnp.where(i[:, None] == i[None, :], 1.0 / DH_A, 0.0).astype(bf16)


def _layer(l, lp, x, b, seq, rope_tab, states, sl, mem_k, mem_v, ml, mem_tall, paged=None):
    lru_h0, lru_buf, ssm_h0, ssm_buf = states
    lam_init = 0.8 - 0.6 * math.exp(-0.3 * l)
    proj, dt_raw = _in_proj(x, lp['norm_mix'], lp['w_main'], lp['w_dt'])
    cos, sin = rope_tab
    act = f32 if paged is not None else bf16
    q_scale = DH_A ** -0.5 * (1.0 if paged is not None else math.log2(math.e))
    q, k, kb, vb = _qk_prep(proj, cos, sin, lp['q_gain'], lp['k_gain'], _head_bd(), act, q_scale)
    if paged is None:
        y_a = _attn_prompt(lp['lambda_qk'], q, kb, vb, lp['subln'], lam_init, b, seq)
    else:
        page_table, cache_kt, cache_v2 = paged
        y_a = _attn_sample(page_table, lp['lambda_qk'], q, kb, vb, lp['subln'], cache_kt, cache_v2, l, lam_init, b, seq)
    y_b, lru_h = _lru(proj, lru_buf, lru_h0, lp['lru_cw'], lp['lru_cb'], lp['lru_wa'], lp['lru_ba'], lp['lru_wx'],
                      lp['lru_bx'], lp['lru_lambda'], sl, b, seq)
    sgu_w, sgu_b = _sgu_weights(lp, seq)
    y_c, vc = _cmix(proj, lp['sgu_norm'], sgu_w, sgu_b)
    y_d, ssm_h = _ssd(proj, dt_raw, ssm_buf, ssm_h0, lp['ssd_cw'], lp['ssd_cb'], lp['ssd_dtb'], lp['ssd_alog'],
                      lp['ssd_d'], lp['ssd_norm'], sl, b, seq)
    gates = _gate_proj(x, lp['norm_mix'], lp['w_gates'])
    x = _merge(x, y_a, y_b, y_c, y_d, gates, lp['w_branch'], lp['w_o'])
    x = _cross(x, lp['norm_cross'], lp['w_xq'], lp['xq_norm'], mem_k, mem_v, lp['w_xo'], ml, seq, mem_tall)
    x = _ffn(x, lp['norm_ffn'], lp['w_ffn_in'], lp['w_ffn_out'])
    p3 = proj.reshape(b, seq, N_MAIN)
    v = p3[:, :, COL_V * 512:(COL_V + 1) * 512]
    assert seq >= CONV_W - 1
    new_lru_buf = p3[:, seq - (CONV_W - 1):, COL_XB * 512:(COL_XB + 1) * 512]
    new_ssm_buf = p3[:, seq - (CONV_W - 1):, COL_XBC * 1024:(COL_XBC + 1) * 1024]
    return x, (k, v, lru_h, new_lru_buf, ssm_h, new_ssm_buf, vc)


def kernel(x_prompt, x_sample, mem_prompt, cache_attn_k, cache_attn_v, page_table, cache_mem_k, cache_mem_v, state_lru, state_lru_conv, state_ssm, state_ssm_conv, norm_mix, w_in, q_norm, k_norm, lambda_qk, subln, lru_conv_w, lru_conv_b, lru_wa, lru_ba, lru_wx, lru_bx, lru_lambda, sgu_norm, sgu_w, sgu_b, ssd_conv_w, ssd_conv_b, ssd_dt_bias, ssd_a_log, ssd_d, ssd_norm, w_branch, w_o, norm_cross, w_xq, w_xk, w_xv, xq_norm, xk_norm, w_xo, norm_ffn, w_ffn_in, w_ffn_out):
    P = dict(norm_mix=norm_mix, w_in=w_in, q_norm=q_norm, k_norm=k_norm, lambda_qk=lambda_qk, subln=subln,
             lru_conv_w=lru_conv_w, lru_conv_b=lru_conv_b, lru_wa=lru_wa, lru_ba=lru_ba, lru_wx=lru_wx, lru_bx=lru_bx,
             lru_lambda=lru_lambda, sgu_norm=sgu_norm, sgu_w=sgu_w, sgu_b=sgu_b, ssd_conv_w=ssd_conv_w,
             ssd_conv_b=ssd_conv_b, ssd_dt_bias=ssd_dt_bias, ssd_a_log=ssd_a_log, ssd_d=ssd_d, ssd_norm=ssd_norm,
             w_branch=w_branch, w_o=w_o, norm_cross=norm_cross, w_xq=w_xq, xq_norm=xq_norm, w_xo=w_xo,
             norm_ffn=norm_ffn, w_ffn_in=w_ffn_in, w_ffn_out=w_ffn_out)
    bp, lp_, _ = x_prompt.shape
    bs, ls, _ = x_sample.shape
    n_pages = page_table.shape[1]
    past_len = n_pages * PAGE_SIZE
    n_phys = cache_attn_k.shape[1]

    rope_p = _rope_tables(jnp.arange(lp_))
    cos_s, sin_s = _rope_tables(past_len + jnp.arange(ls))
    rope_s = (jnp.tile(cos_s, (bs, 1)), jnp.tile(sin_s, (bs, 1)))
    cache_kt = jnp.transpose(cache_attn_k, (0, 1, 3, 4, 2))
    cache_v2 = cache_attn_v.reshape(DEPTH, n_phys, PAGE_SIZE * H_A, DV_A)
    mem_k_s = cache_mem_k.reshape(DEPTH, bs, N_MEM * XH, XDH)
    mem_v_s = cache_mem_v.reshape(DEPTH, bs, N_MEM * XH, XDH)

    pmk, pmv = _memkv(mem_prompt.reshape(bp * N_MEM, D_MODEL), w_xk.astype(bf16), w_xv.astype(bf16), xk_norm[:, None, :])

    yp = x_prompt.reshape(bp * lp_, D_MODEL)
    ys = x_sample.reshape(bs * ls, D_MODEL)
    zeros = lambda *s: jnp.zeros(s, f32)
    states_p = (zeros(1, bp, 1, W_B), zeros(1, bp, CONV_W - 1, W_B), zeros(1, bp, H_D * P_D, N_D),
                zeros(1, bp, CONV_W - 1, XBC_D))
    states_s = (state_lru[:, :, None, :], state_lru_conv, state_ssm.reshape(DEPTH, bs, H_D * P_D, N_D), state_ssm_conv)
    pmk4 = pmk.reshape(DEPTH, bp, N_MEM, XW)
    pmv4 = pmv.reshape(DEPTH, bp, N_MEM, XW)
    p_states, s_states = [], []
    for l in range(DEPTH):
        lp = _layer_params(l, P)
        yp, sp = _layer(l, lp, yp, bp, lp_, rope_p, states_p, 0, pmk4, pmv4, l, False)
        p_states.append(sp)
        ys, ss = _layer(l, lp, ys, bs, ls, rope_s, states_s, l, mem_k_s, mem_v_s, l, True,
                        paged=(page_table, cache_kt, cache_v2))
        s_states.append(ss)

    def stack(states, idx, shape):
        return jnp.stack([s[idx] for s in states]).reshape((DEPTH,) + shape)

    return (
        yp.reshape(bp, lp_, D_MODEL), ys.reshape(bs, ls, D_MODEL),
        stack(p_states, 0, (bp, lp_, 2 * H_A, DH_A)), stack(p_states, 1, (bp, lp_, H_A, DV_A)),
        pmk.reshape(DEPTH, bp, N_MEM, XH, XDH), pmv.reshape(DEPTH, bp, N_MEM, XH, XDH),
        stack(p_states, 2, (bp, W_B)), stack(p_states, 3, (bp, CONV_W - 1, W_B)),
        stack(p_states, 4, (bp, H_D, P_D, N_D)), stack(p_states, 5, (bp, CONV_W - 1, XBC_D)),
        stack(s_states, 0, (bs, ls, 2 * H_A, DH_A)), stack(s_states, 1, (bs, ls, H_A, DV_A)),
        stack(s_states, 2, (bs, W_B)), stack(s_states, 3, (bs, CONV_W - 1, W_B)),
        stack(s_states, 4, (bs, H_D, P_D, N_D)), stack(s_states, 5, (bs, CONV_W - 1, XBC_D)),
        stack(s_states, 6, (bs, ls, W_C)),
    )
```

```python
import functools
import math

import jax
import jax.numpy as jnp
from jax import lax
from jax.experimental import pallas as pl
from jax.experimental.pallas import tpu as pltpu

f32 = jnp.float32
bf16 = jnp.bfloat16

D_MODEL = 1024
DEPTH = 4
PAGE_SIZE = 128
H_A = 4
DH_A = 64
DV_A = 128
W_A = 512
ROPE_THETA = 10000.0
W_B = 512
NB_B = 8
LRU_C = 8.0
CONV_W = 4
W_C = 512
G_C = 4
CHUNK_C = 128
H_D = 8
P_D = 64
W_D = 512
G_D = 2
N_D = 128
SSD_CHUNK = 128
XBC_D = 1024
N_MEM = 256
XH = 4
XDH = 128
XW = 512
D_FF = 2816
EPS = 1e-6

LANES = 128
SUBLANES = 8
VMEM_LIMIT = 48 * 1024 * 1024

N_BRANCH = 4
N_MAIN = 5120
COL_Q, COL_K, COL_V, COL_XB, COL_GB, COL_UC, COL_VC, COL_ZD = range(8)
COL_XBC = 4


ROW_TILE = 512


def _cparams(sem):
    return pltpu.CompilerParams(dimension_semantics=sem, vmem_limit_bytes=VMEM_LIMIT)


def _resident(shape, index_map):
    return pl.BlockSpec(shape, index_map, pipeline_mode=pl.Buffered(1))


def _dot(a, b):
    return jnp.dot(a, b, preferred_element_type=f32)


def _dot_nt(a, b):
    return lax.dot_general(a, b, (((1,), (1,)), ((), ())), preferred_element_type=f32)


def _dot_tn(a, b):
    return lax.dot_general(a, b, (((0,), (0,)), ((), ())), preferred_element_type=f32)


def _rms(x, g):
    ms = jnp.mean(x * x, axis=-1, keepdims=True)
    return x * lax.rsqrt(ms + EPS) * g


def _softplus(x):
    return jnp.maximum(x, 0.0) + jnp.log1p(jnp.exp(-jnp.abs(x)))


def _in_proj_body(x_ref, g_ref, w_ref, wdt_ref, proj_ref, dt_ref):
    hn = _rms(x_ref[...], g_ref[...]).astype(bf16)
    dt_ref[...] = _dot(hn, wdt_ref[...])
    for c in range(N_MAIN // 512):
        sl = slice(c * 512, (c + 1) * 512)
        acc = _dot(hn, w_ref[:, sl])
        if COL_GB <= c <= COL_VC:
            acc = jax.nn.gelu(acc)
        elif c == COL_ZD:
            acc = acc * jax.nn.sigmoid(acc)
        proj_ref[:, sl] = acc


def _in_proj(x, g, w_main, w_dt):
    t = x.shape[0]
    tm = min(t, ROW_TILE)
    return pl.pallas_call(
        _in_proj_body,
        grid=(t // tm,),
        in_specs=[
            pl.BlockSpec((tm, D_MODEL), lambda i: (i, 0)),
            _resident((1, D_MODEL), lambda i: (0, 0)),
            _resident((D_MODEL, N_MAIN), lambda i: (0, 0)),
            _resident((D_MODEL, LANES), lambda i: (0, 0)),
        ],
        out_specs=[
            pl.BlockSpec((tm, N_MAIN), lambda i: (i, 0)),
            pl.BlockSpec((tm, LANES), lambda i: (i, 0)),
        ],
        out_shape=[jax.ShapeDtypeStruct((t, N_MAIN), f32), jax.ShapeDtypeStruct((t, LANES), f32)],
        compiler_params=_cparams(("parallel",)),
        name="in_proj",
    )(x, g, w_main, w_dt)


def _head_norm_rope(x, gain, cos, sin, bd):
    sq = x * x
    hi = sq.astype(bf16)
    lo = (sq - hi.astype(f32)).astype(bf16)
    ms = _dot(hi, bd) + _dot(lo, bd)
    y = x * lax.rsqrt(ms + EPS) * gain
    lane = lax.broadcasted_iota(jnp.int32, y.shape, 1)
    first_half = (lane % DH_A) < (DH_A // 2)
    swapped = jnp.where(first_half, pltpu.roll(y, LANES - DH_A // 2, 1), pltpu.roll(y, DH_A // 2, 1))
    return y * cos + swapped * sin


def _qk_prep_body(q_ref, k_ref, v_ref, cos_ref, sin_ref, qg_ref, kg_ref, bd_ref, qo_ref, ko_ref, kb_ref, vb_ref, vt_ref,
                  *, q_scale):
    bd = bd_ref[...]
    rows = v_ref.shape[0]
    for h in range(H_A):
        vt_ref[pl.ds(h, rows, stride=H_A), :] = v_ref[:, h * DV_A:(h + 1) * DV_A]
    for c in range(W_A // LANES):
        sl = slice(c * LANES, (c + 1) * LANES)
        cos = cos_ref[:, sl]
        sin = sin_ref[:, sl]
        q = _head_norm_rope(q_ref[:, sl], qg_ref[...], cos, sin, bd)
        k = _head_norm_rope(k_ref[:, sl], kg_ref[...], cos, sin, bd)
        qo_ref[:, sl] = (q * q_scale).astype(qo_ref.dtype)
        ko_ref[:, sl] = k
        kb_ref[:, sl] = k.astype(kb_ref.dtype)
    vb_ref[...] = v_ref[...].astype(vb_ref.dtype)


def _qk_prep(proj, cos, sin, qg, kg, bd, act_dtype, q_scale):
    t = proj.shape[0]
    p = cos.shape[0]
    tm = min(t, p, 512)
    npos = p // tm
    row = lambda c: pl.BlockSpec((tm, 512), lambda i, c=c: (i, c))
    tab = pl.BlockSpec((tm, 512), lambda i: (i % npos, 0))
    small = lambda shape: pl.BlockSpec(shape, lambda i: (0, 0))
    return pl.pallas_call(
        functools.partial(_qk_prep_body, q_scale=q_scale),
        grid=(t // tm,),
        in_specs=[row(COL_Q), row(COL_K), row(COL_V), tab, tab, small((1, LANES)), small((1, LANES)),
                  small((LANES, LANES))],
        out_specs=[row(0)] * 4 + [pl.BlockSpec((tm * H_A, DV_A), lambda i: (i, 0))],
        out_shape=[jax.ShapeDtypeStruct((t, 512), act_dtype), jax.ShapeDtypeStruct((t, 512), f32),
                   jax.ShapeDtypeStruct((t, 512), act_dtype), jax.ShapeDtypeStruct((t, 512), act_dtype),
                   jax.ShapeDtypeStruct((t * H_A, DV_A), f32)],
        compiler_params=_cparams(("parallel",)),
        name="qk_prep",
    )(proj, proj, proj, cos, sin, qg, kg, bd)


def _lambda_value(lq, lam_init):
    a = jnp.sum(lq[0:1] * lq[1:2], axis=-1, keepdims=True)
    b = jnp.sum(lq[2:3] * lq[3:4], axis=-1, keepdims=True)
    return jnp.exp(a) - jnp.exp(b) + lam_init


def _stack_pair(qp):
    lane = lax.broadcasted_iota(jnp.int32, qp.shape, 1)
    zero = jnp.zeros_like(qp)
    return jnp.concatenate([jnp.where(lane < DH_A, qp, zero), jnp.where(lane >= DH_A, qp, zero)], axis=0)


def _diff_finish(acc, l, lam, sub, lam_init, tq):
    o = acc[:tq] / l[:tq] - lam * (acc[tq:] / l[tq:])
    return _rms(o, sub) * (1.0 - lam_init)


def _attn_prompt_body(lq_ref, q_ref, k_ref, v_ref, sub_ref, o_ref, *, lam_init, tq, tk):
    qi = pl.program_id(1)
    lam = _lambda_value(lq_ref[...], lam_init)
    per = tk // tq
    n_full = qi // per
    r = lax.broadcasted_iota(jnp.int32, (2 * tq, tk), 0)
    r = jnp.where(r >= tq, r - tq, r) + (qi % per) * tq
    c = lax.broadcasted_iota(jnp.int32, (2 * tq, tk), 1)
    diag = c <= r
    sls = [slice(h * LANES, (h + 1) * LANES) for h in range(H_A)]
    qqs = [_stack_pair(q_ref[:, sl]) for sl in sls]

    def step(kv, carry, masked):
        rows = pl.ds(pl.multiple_of(kv * tk, tk), tk)
        out = []
        for (m, l, acc), qq, sl in zip(carry, qqs, sls):
            s = _dot_nt(qq, k_ref[rows, sl])
            if masked:
                s = jnp.where(diag, s, -jnp.inf)
            m_new = jnp.maximum(m, jnp.max(s, axis=-1, keepdims=True))
            alpha = jnp.exp2(m - m_new)
            p = jnp.exp2(s - m_new)
            l = alpha * l + jnp.sum(p, axis=-1, keepdims=True)
            acc = alpha * acc + _dot(p.astype(bf16), v_ref[rows, sl])
            out.append((m_new, l, acc))
        return tuple(out)

    init = tuple((jnp.full((2 * tq, 1), -jnp.inf, f32), jnp.zeros((2 * tq, 1), f32), jnp.zeros((2 * tq, LANES), f32))
                 for _ in range(H_A))
    carry = lax.fori_loop(0, n_full, functools.partial(step, masked=False), init)
    carry = step(n_full, carry, True)
    for (_, l, acc), sl in zip(carry, sls):
        o_ref[:, sl] = _diff_finish(acc, l, lam, sub_ref[:, sl], lam_init, tq)


ATTN_TQ = 256
ATTN_TK = 512


def _attn_prompt(lq, q, k, v, sub, lam_init, b, seq):
    tq = min(seq, ATTN_TQ)
    tk = min(seq, ATTN_TK)
    nq = seq // tq
    return pl.pallas_call(
        functools.partial(_attn_prompt_body, lam_init=lam_init, tq=tq, tk=tk),
        grid=(b, nq),
        in_specs=[
            pl.BlockSpec((4, DH_A), lambda i, j: (0, 0)),
            pl.BlockSpec((tq, 512), lambda i, j: (i * nq + j, 0)),
            pl.BlockSpec((seq, 512), lambda i, j: (i, 0)),
            pl.BlockSpec((seq, 512), lambda i, j: (i, 0)),
            pl.BlockSpec((1, 512), lambda i, j: (0, 0)),
        ],
        out_specs=pl.BlockSpec((tq, 512), lambda i, j: (i * nq + j, 0)),
        out_shape=jax.ShapeDtypeStruct((b * seq, 512), f32),
        compiler_params=_cparams(("parallel", "arbitrary")),
        name="attn_prompt",
    )(lq, q, k, v, sub)


def _attn_sample_body(pt_ref, lq_ref, q_ref, kn_ref, vn_ref, sub_ref, *rest, lam_init, n_pages, ls):
    del pt_ref
    k_refs = rest[:n_pages]
    v_refs = rest[n_pages:2 * n_pages]
    o_ref = rest[2 * n_pages]
    lam = _lambda_value(lq_ref[...], lam_init)
    pad = jnp.zeros((PAGE_SIZE - ls, LANES), f32)
    r = lax.broadcasted_iota(jnp.int32, (2 * ls, PAGE_SIZE), 0)
    r = jnp.where(r >= ls, r - ls, r)
    c = lax.broadcasted_iota(jnp.int32, (2 * ls, PAGE_SIZE), 1)
    new_mask = c <= r
    for h in range(H_A):
        sl = slice(h * LANES, (h + 1) * LANES)
        qq = _stack_pair(q_ref[:, sl]).astype(bf16)
        kt = jnp.concatenate([k_refs[p][2 * h:2 * h + 2].reshape(LANES, PAGE_SIZE) for p in range(n_pages)], axis=1)
        s_past = _dot(qq, kt.astype(bf16))
        k_new = jnp.concatenate([kn_ref[:, sl], pad], axis=0).astype(bf16)
        s_new = jnp.where(new_mask, _dot_nt(qq, k_new), -jnp.inf)
        m = jnp.maximum(jnp.max(s_past, axis=-1, keepdims=True), jnp.max(s_new, axis=-1, keepdims=True))
        p_past = jnp.exp(s_past - m)
        p_new = jnp.exp(s_new - m)
        l = jnp.sum(p_past, axis=-1, keepdims=True) + jnp.sum(p_new, axis=-1, keepdims=True)
        v_past = jnp.concatenate([v_refs[p][pl.ds(h, PAGE_SIZE, stride=H_A), :] for p in range(n_pages)], axis=0)
        v_new = jnp.concatenate([vn_ref[:, sl], pad], axis=0).astype(bf16)
        acc = _dot(p_past.astype(bf16), v_past.astype(bf16)) + _dot(p_new.astype(bf16), v_new)
        o_ref[:, sl] = _diff_finish(acc, l, lam, sub_ref[:, sl], lam_init, ls)


def _attn_sample(page_table, lq, q, k_new, v_new, sub, cache_kt, cache_v2, layer, lam_init, b, ls):
    n_pages = page_table.shape[1]
    row = pl.BlockSpec((ls, 512), lambda i, pt: (i, 0))
    k_specs = [pl.BlockSpec((None, None, 2 * H_A, DH_A, PAGE_SIZE), lambda i, pt, p=p: (layer, pt[i, p], 0, 0, 0))
               for p in range(n_pages)]
    v_specs = [pl.BlockSpec((None, None, PAGE_SIZE * H_A, DV_A), lambda i, pt, p=p: (layer, pt[i, p], 0, 0))
               for p in range(n_pages)]
    grid_spec = pltpu.PrefetchScalarGridSpec(
        num_scalar_prefetch=1,
        grid=(b,),
        in_specs=[pl.BlockSpec((4, DH_A), lambda i, pt: (0, 0)), row, row, row,
                  pl.BlockSpec((1, 512), lambda i, pt: (0, 0))] + k_specs + v_specs,
        out_specs=row,
    )
    return pl.pallas_call(
        functools.partial(_attn_sample_body, lam_init=lam_init, n_pages=n_pages, ls=ls),
        grid_spec=grid_spec,
        out_shape=jax.ShapeDtypeStruct((b * ls, 512), f32),
        compiler_params=_cparams(("parallel",)),
        name="attn_sample",
    )(page_table, lq, q, k_new, v_new, sub, *([cache_kt] * n_pages), *([cache_v2] * n_pages))


def _conv_tile(x_ref, buf_ref, w_ref, b_ref, xx_ref, first, tt):
    @pl.when(first)
    def _():
        xx_ref[SUBLANES - (CONV_W - 1):SUBLANES, :] = buf_ref[...]

    @pl.when(jnp.logical_not(first))
    def _():
        xx_ref[SUBLANES - (CONV_W - 1):SUBLANES, :] = xx_ref[tt + SUBLANES - (CONV_W - 1):tt + SUBLANES, :]

    xx_ref[SUBLANES:SUBLANES + tt, :] = x_ref[...]
    y = b_ref[...] + w_ref[0:1, :] * xx_ref[SUBLANES - 3:SUBLANES - 3 + tt, :]
    for j in range(1, CONV_W):
        y = y + w_ref[j:j + 1, :] * xx_ref[SUBLANES - 3 + j:SUBLANES - 3 + j + tt, :]
    return y


def _lru_body(x_ref, gate_ref, buf_ref, h0_ref, cw_ref, cb_ref, wa_ref, ba_ref, wx_ref, bx_ref, lam_ref,
              y_ref, hf_ref, xx_ref, a_ref, u_ref, h_ref, *, n_seq, tt):
    ti = pl.program_id(1)
    xc = [_conv_tile(x_ref.at[pl.ds(s * tt, tt)], buf_ref.at[s], cw_ref, cb_ref, xx_ref.at[s], ti == 0, tt)
          for s in range(n_seq)]
    xc = xc[0] if n_seq == 1 else jnp.concatenate(xc, axis=0)
    xcb = xc.astype(bf16)
    r = jax.nn.sigmoid(_dot(xcb, wa_ref[...]) + ba_ref[...])
    i = jax.nn.sigmoid(_dot(xcb, wx_ref[...]) + bx_ref[...])
    log_a = (-LRU_C) * r * _softplus(-lam_ref[...])
    a_ref[...] = jnp.exp(log_a)
    th = jnp.tanh(log_a)
    u_ref[...] = jnp.sqrt(-2.0 * th / (1.0 - th)) * (i * xc)

    @pl.when(ti == 0)
    def _():
        h_ref[...] = h0_ref[...]

    row = lax.broadcasted_iota(jnp.int32, (SUBLANES, W_B), 0)

    def tile(start, h):
        rows = pl.ds(pl.multiple_of(start, SUBLANES), SUBLANES)
        a = a_ref[rows, :]
        u = u_ref[rows, :]
        for sh in (1, 2, 4):
            keep = row >= sh
            u = jnp.where(keep, a * pltpu.roll(u, sh, 0) + u, u)
            a = jnp.where(keep, a * pltpu.roll(a, sh, 0), a)
        hh = a * h + u
        y_ref[rows, :] = hh * gate_ref[rows, :]
        return hh[SUBLANES - 1:SUBLANES, :]

    for s in range(n_seq):
        if tt == SUBLANES:
            h = tile(s * tt, h_ref[s])
        else:
            h = lax.fori_loop(0, tt // SUBLANES, lambda k, h, s=s: tile(s * tt + k * SUBLANES, h), h_ref[s])
        h_ref[s] = h
        hf_ref[s] = h


def _lru(proj, buf, h0, cw, cb, wa, ba, wx, bx, lam, layer, b, seq):
    n_seq, tt = (1, min(seq, 512)) if seq >= LANES else (LANES // seq, seq)
    nt = seq // tt
    rows = n_seq * tt
    small = lambda shape: pl.BlockSpec(shape, lambda i, j: (0,) * len(shape))
    return pl.pallas_call(
        functools.partial(_lru_body, n_seq=n_seq, tt=tt),
        grid=(b // n_seq, nt),
        in_specs=[
            pl.BlockSpec((rows, 512), lambda i, j: (i * nt + j, COL_XB)),
            pl.BlockSpec((rows, 512), lambda i, j: (i * nt + j, COL_GB)),
            pl.BlockSpec((None, n_seq, CONV_W - 1, W_B), lambda i, j: (layer, i, 0, 0)),
            pl.BlockSpec((None, n_seq, 1, W_B), lambda i, j: (layer, i, 0, 0)),
            small((CONV_W, W_B)), small((1, W_B)), small((W_B, W_B)), small((1, W_B)), small((W_B, W_B)),
            small((1, W_B)), small((1, W_B)),
        ],
        out_specs=[
            pl.BlockSpec((rows, 512), lambda i, j: (i * nt + j, 0)),
            pl.BlockSpec((n_seq, 1, W_B), lambda i, j: (i, 0, 0)),
        ],
        out_shape=[jax.ShapeDtypeStruct((b * seq, W_B), f32), jax.ShapeDtypeStruct((b, 1, W_B), f32)],
        scratch_shapes=[pltpu.VMEM((n_seq, tt + SUBLANES, W_B), f32), pltpu.VMEM((rows, W_B), f32),
                        pltpu.VMEM((rows, W_B), f32), pltpu.VMEM((n_seq, 1, W_B), f32)],
        compiler_params=_cparams(("parallel", "arbitrary")),
        name="lru",
    )(proj, proj, buf, h0, cw, cb, wa, ba, wx, bx, lam)


def _cmix_body(u_ref, v_ref, g_ref, w_ref, b_ref, y_ref, vn_ref, *, n_chunks):
    vn = _rms(v_ref[...], g_ref[...])
    vn_ref[...] = vn
    r = lax.broadcasted_iota(jnp.int32, (CHUNK_C, CHUNK_C), 0)
    c = lax.broadcasted_iota(jnp.int32, (CHUNK_C, CHUNK_C), 1)
    for g in range(G_C):
        sl = slice(g * LANES, (g + 1) * LANES)
        wg = jnp.where(c <= r, w_ref[g], 0.0).astype(bf16)
        bias = b_ref[:, g:g + 1]
        for ch in range(n_chunks):
            rows = slice(ch * CHUNK_C, (ch + 1) * CHUNK_C)
            mixed = _dot(wg, vn[rows, sl].astype(bf16)) + bias
            y_ref[rows, sl] = u_ref[rows, sl] * mixed


def _cmix(proj, g, w, bias):
    t = proj.shape[0]
    tc = min(t, 512)
    return pl.pallas_call(
        functools.partial(_cmix_body, n_chunks=tc // CHUNK_C),
        grid=(t // tc,),
        in_specs=[
            pl.BlockSpec((tc, 512), lambda i: (i, COL_UC)),
            pl.BlockSpec((tc, 512), lambda i: (i, COL_VC)),
            pl.BlockSpec((1, W_C), lambda i: (0, 0)),
            pl.BlockSpec((G_C, CHUNK_C, CHUNK_C), lambda i: (0, 0, 0)),
            pl.BlockSpec((CHUNK_C, G_C), lambda i: (0, 0)),
        ],
        out_specs=[pl.BlockSpec((tc, 512), lambda i: (i, 0))] * 2,
        out_shape=[jax.ShapeDtypeStruct((t, W_C), f32)] * 2,
        compiler_params=_cparams(("parallel",)),
        name="cmix",
    )(proj, proj, g, w, bias)


def _ssd_body(xbc_ref, z_ref, dt_ref, buf_ref, h0_ref, cw_ref, cb_ref, dtb_ref, alog_ref, d_ref, g_ref,
              y_ref, hf_ref, xx_ref, st_ref, *, n_seq, rows):
    ci = pl.program_id(1)
    q = SSD_CHUNK

    @pl.when(ci == 0)
    def _():
        st_ref[...] = h0_ref[...]

    xbc = [_conv_tile(xbc_ref.at[pl.ds(s * rows, rows)], buf_ref.at[s], cw_ref, cb_ref, xx_ref.at[s], ci == 0, rows)
           for s in range(n_seq)]
    xbc = xbc[0] if n_seq == 1 else jnp.concatenate(xbc, axis=0)
    xbc = xbc * jax.nn.sigmoid(xbc)
    dtx = dt_ref[...] + dtb_ref[...]
    dt = jnp.maximum(dtx, 0.0) + jnp.log(1.0 + jnp.exp(-jnp.abs(dtx)))
    xs = xbc[:, :W_D]
    a_neg = -jnp.exp(alog_ref[...])
    ri = lax.broadcasted_iota(jnp.int32, (q, q), 0)
    cj = lax.broadcasted_iota(jnp.int32, (q, q), 1)
    causal = cj <= ri
    if n_seq > 1:
        causal = causal & ((ri // rows) == (cj // rows))
    exact_dot = functools.partial(jnp.dot, preferred_element_type=f32, precision=lax.Precision.HIGHEST)
    cum = exact_dot(jnp.where(causal, 1.0, 0.0).astype(f32), dt * a_neg)
    cum_t = cum.T
    if n_seq > 1:
        last_rows = exact_dot(jnp.where(cj == (ri // rows) * rows + (rows - 1), 1.0, 0.0).astype(f32), cum)
    else:
        last_rows = cum[q - 1:q, :]
    lane = lax.broadcasted_iota(jnp.int32, (q, LANES), 1)
    low = lane < P_D
    row_seq = lax.broadcasted_iota(jnp.int32, (q, LANES), 0) // rows
    rlow = lax.broadcasted_iota(jnp.int32, (LANES, N_D), 0) < P_D
    y_parts = []
    for g in range(G_D):
        bm = xbc[:, W_D + g * N_D:W_D + (g + 1) * N_D].astype(bf16)
        cm_f = xbc[:, W_D + (G_D + g) * N_D:W_D + (G_D + g + 1) * N_D]
        cm = cm_f.astype(bf16)
        scores = _dot_nt(cm, bm)
        for k in range(H_D // G_D // 2):
            h0 = g * (H_D // G_D) + 2 * k
            sl = slice(h0 * P_D, (h0 + 2) * P_D)
            col = lambda a, h: a[:, h:h + 1]
            pair = lambda a: jnp.where(low, col(a, h0), col(a, h0 + 1))
            xdt = xs[:, sl] * pair(dt)
            xdt_b = xdt.astype(bf16)
            yd = []
            for h in (h0, h0 + 1):
                seg = col(cum, h) - cum_t[h:h + 1, :]
                m = scores * jnp.exp(jnp.where(causal, seg, -jnp.inf))
                yd.append(_dot(m.astype(bf16), xdt_b))
            xw = xdt * jnp.exp(pair(last_rows - cum))
            offs = []
            for s in range(n_seq):
                rs = slice(s * rows, (s + 1) * rows)
                st = st_ref[s, sl, :]
                offs.append(_dot_nt(cm if n_seq == 1 else cm_f[rs].astype(bf16), st.astype(bf16)))
                last = cum[(s + 1) * rows - 1:(s + 1) * rows, :]
                dec = jnp.exp(jnp.where(rlow, last[:, h0:h0 + 1], last[:, h0 + 1:h0 + 2]))
                xw_s = xw if n_seq == 1 else jnp.where(row_seq == s, xw, 0.0)
                st_ref[s, sl, :] = dec * st + _dot_tn(xw_s.astype(bf16), bm)
            y_off = (offs[0] if n_seq == 1 else jnp.concatenate(offs, axis=0)) * jnp.exp(pair(cum))
            y_parts.append(jnp.where(low, yd[0], yd[1]) + y_off + d_ref[:, sl] * xs[:, sl])
    y = jnp.concatenate(y_parts, axis=1) * z_ref[...]
    y_ref[...] = _rms(y, g_ref[...])
    hf_ref[...] = st_ref[...]


def _ssd(proj, dt_raw, buf, h0, cw, cb, dtb, alog, dvec, g, layer, b, seq):
    n_seq, rows = (1, SSD_CHUNK) if seq >= SSD_CHUNK else (SSD_CHUNK // seq, seq)
    nc = seq // rows
    q = SSD_CHUNK
    small = lambda shape: pl.BlockSpec(shape, lambda i, j: (0,) * len(shape))
    return pl.pallas_call(
        functools.partial(_ssd_body, n_seq=n_seq, rows=rows),
        grid=(b // n_seq, nc),
        in_specs=[
            pl.BlockSpec((q, XBC_D), lambda i, j: (i * nc + j, COL_XBC)),
            pl.BlockSpec((q, 512), lambda i, j: (i * nc + j, COL_ZD)),
            pl.BlockSpec((q, LANES), lambda i, j: (i * nc + j, 0)),
            pl.BlockSpec((None, n_seq, CONV_W - 1, XBC_D), lambda i, j: (layer, i, 0, 0)),
            pl.BlockSpec((None, n_seq, H_D * P_D, N_D), lambda i, j: (layer, i, 0, 0)),
            small((CONV_W, XBC_D)), small((1, XBC_D)), small((1, LANES)), small((1, LANES)), small((1, W_D)),
            small((1, W_D)),
        ],
        out_specs=[
            pl.BlockSpec((q, W_D), lambda i, j: (i * nc + j, 0)),
            pl.BlockSpec((n_seq, H_D * P_D, N_D), lambda i, j: (i, 0, 0)),
        ],
        out_shape=[jax.ShapeDtypeStruct((b * seq, W_D), f32), jax.ShapeDtypeStruct((b, H_D * P_D, N_D), f32)],
        scratch_shapes=[pltpu.VMEM((n_seq, rows + SUBLANES, XBC_D), f32), pltpu.VMEM((n_seq, H_D * P_D, N_D), f32)],
        compiler_params=_cparams(("parallel", "arbitrary")),
        name="ssd",
    )(proj, proj, dt_raw, buf, h0, cw, cb, dtb, alog, dvec, g)


def _merge_body(x_ref, g_ref, ya_ref, yb_ref, yc_ref, yd_ref, wg_ref, wb_ref, wo_ref, o_ref):
    x = x_ref[...]
    hn = _rms(x, g_ref[...]).astype(bf16)
    m = None
    for k, y_ref in enumerate((ya_ref, yb_ref, yc_ref, yd_ref)):
        gate = jax.nn.sigmoid(_dot(hn, wg_ref[:, k * D_MODEL:(k + 1) * D_MODEL]))
        t = gate * _dot(y_ref[...].astype(bf16), wb_ref[k])
        m = t if m is None else m + t
    o_ref[...] = x + _dot(m.astype(bf16), wo_ref[...])


def _merge(x, g, ya, yb, yc, yd, wg, wb, wo):
    t = x.shape[0]
    tm = min(t, ROW_TILE)
    row = lambda w: pl.BlockSpec((tm, w), lambda i: (i, 0))
    return pl.pallas_call(
        _merge_body,
        grid=(t // tm,),
        in_specs=[row(D_MODEL), _resident((1, D_MODEL), lambda i: (0, 0)), row(512), row(512), row(512), row(512),
                  _resident((D_MODEL, N_BRANCH * D_MODEL), lambda i: (0, 0)),
                  _resident((N_BRANCH, 512, D_MODEL), lambda i: (0, 0, 0)),
                  _resident((D_MODEL, D_MODEL), lambda i: (0, 0))],
        out_specs=row(D_MODEL),
        out_shape=jax.ShapeDtypeStruct((t, D_MODEL), f32),
        compiler_params=_cparams(("parallel",)),
        name="merge",
    )(x, g, ya, yb, yc, yd, wg, wb, wo)


def _memkv_body(m_ref, wk_ref, wv_ref, g_ref, k_ref, v_ref, kt_ref, vt_ref):
    mb = m_ref[...].astype(bf16)
    rows = mb.shape[0]
    kk = _dot(mb, wk_ref[...])
    vv = _dot(mb, wv_ref[...])
    v_ref[...] = vv
    for h in range(XH):
        sl = slice(h * XDH, (h + 1) * XDH)
        kh = _rms(kk[:, sl], g_ref[...])
        k_ref[:, sl] = kh
        kt_ref[pl.ds(h, rows, stride=XH), :] = kh
        vt_ref[pl.ds(h, rows, stride=XH), :] = vv[:, sl]


def _memkv(mem, wk, wv, g):
    r = mem.shape[0]
    tm = min(r, 512)
    return pl.pallas_call(
        _memkv_body,
        grid=(DEPTH, r // tm),
        in_specs=[pl.BlockSpec((tm, D_MODEL), lambda l, i: (i, 0)),
                  pl.BlockSpec((None, D_MODEL, XW), lambda l, i: (l, 0, 0)),
                  pl.BlockSpec((None, D_MODEL, XW), lambda l, i: (l, 0, 0)),
                  pl.BlockSpec((None, 1, XDH), lambda l, i: (l, 0, 0))],
        out_specs=[pl.BlockSpec((None, tm, XW), lambda l, i: (l, i, 0))] * 2
        + [pl.BlockSpec((None, tm * XH, XDH), lambda l, i: (l, i, 0))] * 2,
        out_shape=[jax.ShapeDtypeStruct((DEPTH, r, XW), f32)] * 2 + [jax.ShapeDtypeStruct((DEPTH, r * XH, XDH), f32)] * 2,
        compiler_params=_cparams(("parallel", "parallel")),
        name="memkv",
    )(mem, wk, wv, g)


def _cross_body(x_ref, g_ref, wq_ref, qg_ref, k_ref, v_ref, wo_ref, o_ref, *, n_seq, rows, tall):
    x = x_ref[...]
    qx = _dot(_rms(x, g_ref[...]).astype(bf16), wq_ref[...])
    qn = [_rms(qx[:, h * XDH:(h + 1) * XDH], qg_ref[...]) for h in range(XH)]

    def softmax(sc):
        p = jnp.exp(sc - jnp.max(sc, axis=-1, keepdims=True))
        return (p / jnp.sum(p, axis=-1, keepdims=True)).astype(bf16)

    if tall:
        r = lax.broadcasted_iota(jnp.int32, (XH * rows, N_MEM * XH), 0)
        c = lax.broadcasted_iota(jnp.int32, (XH * rows, N_MEM * XH), 1)
        same_head = (c % XH) == (r // rows)
        outs = []
        for s in range(n_seq):
            qs = jnp.concatenate([q[s * rows:(s + 1) * rows] for q in qn], axis=0)
            sc = _dot_nt(qs.astype(bf16), k_ref[s].astype(bf16)) * (XDH ** -0.5)
            o = _dot(softmax(jnp.where(same_head, sc, -jnp.inf)), v_ref[s].astype(bf16))
            outs.append(jnp.concatenate([o[h * rows:(h + 1) * rows] for h in range(XH)], axis=1))
        ox = jnp.concatenate(outs, axis=0)
    else:
        heads = []
        for h in range(XH):
            sl = slice(h * XDH, (h + 1) * XDH)
            outs = []
            for s in range(n_seq):
                sc = _dot_nt(qn[h][s * rows:(s + 1) * rows].astype(bf16), k_ref[s, :, sl].astype(bf16)) * (XDH ** -0.5)
                outs.append(_dot(softmax(sc), v_ref[s, :, sl].astype(bf16)))
            heads.append(outs[0] if n_seq == 1 else jnp.concatenate(outs, axis=0))
        ox = jnp.concatenate(heads, axis=1)
    o_ref[...] = x + _dot(ox.astype(bf16), wo_ref[...])


def _cross(x, g, wq, qg, mem_k, mem_v, wo, layer, seq, tall):
    t = x.shape[0]
    if seq >= 512:
        n_seq, rows = 1, 512
        per = seq // rows
        mem_idx = lambda i: (layer, i // per, 0, 0)
    else:
        n_seq, rows = 8, seq
        mem_idx = lambda i: (layer, i, 0, 0)
    tm = n_seq * rows
    small = lambda shape: pl.BlockSpec(shape, lambda i: (0,) * len(shape))
    mem_block = (None, n_seq) + tuple(mem_k.shape[2:])
    return pl.pallas_call(
        functools.partial(_cross_body, n_seq=n_seq, rows=rows, tall=tall),
        grid=(t // tm,),
        in_specs=[pl.BlockSpec((tm, D_MODEL), lambda i: (i, 0)), small((1, D_MODEL)), small((D_MODEL, XW)),
                  small((1, XDH)), pl.BlockSpec(mem_block, mem_idx), pl.BlockSpec(mem_block, mem_idx),
                  small((XW, D_MODEL))],
        out_specs=pl.BlockSpec((tm, D_MODEL), lambda i: (i, 0)),
        out_shape=jax.ShapeDtypeStruct((t, D_MODEL), f32),
        compiler_params=_cparams(("parallel",)),
        name="cross",
    )(x, g, wq, qg, mem_k, mem_v, wo)


def _ffn_body(x_ref, g_ref, wg_ref, wu_ref, wo_ref, o_ref):
    x = x_ref[...]
    hn = _rms(x, g_ref[...]).astype(bf16)
    gate = _dot(hn, wg_ref[...])
    up = _dot(hn, wu_ref[...])
    act = (gate * jax.nn.sigmoid(gate) * up).astype(bf16)
    o_ref[...] = x + _dot(act, wo_ref[...])


def _ffn(x, g, w_in, w_out):
    t = x.shape[0]
    tm = min(t, ROW_TILE)
    return pl.pallas_call(
        _ffn_body,
        grid=(t // tm,),
        in_specs=[pl.BlockSpec((tm, D_MODEL), lambda i: (i, 0)),
                  _resident((1, D_MODEL), lambda i: (0, 0)),
                  _resident((D_MODEL, D_FF), lambda i: (0, 0)),
                  _resident((D_MODEL, D_FF), lambda i: (0, 1)),
                  _resident((D_FF, D_MODEL), lambda i: (0, 0))],
        out_specs=pl.BlockSpec((tm, D_MODEL), lambda i: (i, 0)),
        out_shape=jax.ShapeDtypeStruct((t, D_MODEL), f32),
        compiler_params=_cparams(("parallel",)),
        name="ffn",
    )(x, g, w_in, w_in, w_out)


def _rope_tables(pos):
    half = DH_A // 2
    inv = ROPE_THETA ** (-jnp.arange(half, dtype=f32) / half)
    ang = pos.astype(f32)[:, None] * inv[None, :]
    cos, sin = jnp.cos(ang), jnp.sin(ang)
    cos_h = jnp.concatenate([cos, cos], axis=-1)
    sin_h = jnp.concatenate([-sin, sin], axis=-1)
    reps = W_A // DH_A
    return jnp.tile(cos_h, (1, reps)), jnp.tile(sin_h, (1, reps))


def _block_diag(w):
    nb, a, _ = w.shape
    eye = jnp.eye(nb, dtype=w.dtype)
    return (eye[:, None, :, None] * w[:, :, None, :]).reshape(nb * a, nb * a)


def _layer_params(l, P):
    w_in = P['w_in'][l]
    split_dt = 8 * 512 + XBC_D
    lp = {
        'norm_mix': P['norm_mix'][l][None],
        'w_main': w_in[:, :split_dt].astype(bf16),
        'w_gates': w_in[:, split_dt + H_D:].astype(bf16),
        'w_dt': jnp.pad(w_in[:, split_dt:split_dt + H_D], ((0, 0), (0, LANES - H_D))).astype(bf16),
        'q_gain': jnp.tile(P['q_norm'][l], 2)[None],
        'k_gain': jnp.tile(P['k_norm'][l], 2)[None],
        'lambda_qk': P['lambda_qk'][l],
        'subln': P['subln'][l].reshape(1, W_A),
        'lru_cw': P['lru_conv_w'][l], 'lru_cb': P['lru_conv_b'][l][None],
        'lru_wa': _block_diag(P['lru_wa'][l]).astype(bf16), 'lru_ba': P['lru_ba'][l][None],
        'lru_wx': _block_diag(P['lru_wx'][l]).astype(bf16), 'lru_bx': P['lru_bx'][l][None],
        'lru_lambda': P['lru_lambda'][l][None],
        'sgu_norm': P['sgu_norm'][l][None],
        'sgu_w': P['sgu_w'][l], 'sgu_b': P['sgu_b'][l],
        'ssd_cw': P['ssd_conv_w'][l], 'ssd_cb': P['ssd_conv_b'][l][None],
        'ssd_dtb': jnp.pad(P['ssd_dt_bias'][l], (0, LANES - H_D))[None],
        'ssd_alog': jnp.pad(P['ssd_a_log'][l], (0, LANES - H_D))[None],
        'ssd_d': jnp.repeat(P['ssd_d'][l], P_D)[None],
        'ssd_norm': P['ssd_norm'][l][None],
        'w_branch': P['w_branch'][l].astype(bf16), 'w_o': P['w_o'][l].astype(bf16),
        'norm_cross': P['norm_cross'][l][None], 'w_xq': P['w_xq'][l].astype(bf16),
        'xq_norm': P['xq_norm'][l][None], 'w_xo': P['w_xo'][l].astype(bf16),
        'norm_ffn': P['norm_ffn'][l][None], 'w_ffn_in': P['w_ffn_in'][l].astype(bf16),
        'w_ffn_out': P['w_ffn_out'][l].astype(bf16),
    }
    return lp


def _sgu_weights(lp, seq):
    w, bias = lp['sgu_w'], lp['sgu_b']
    if seq >= CHUNK_C:
        return w, bias.T
    reps = CHUNK_C // seq
    eye = jnp.eye(reps, dtype=w.dtype)
    wk = (eye[None, :, None, :, None] * w[:, None, :seq, None, :seq]).reshape(G_C, CHUNK_C, CHUNK_C)
    return wk, jnp.tile(bias[:, :seq], (1, reps)).T


def _head_bd():
    i = jnp.arange(LANES) // DH_A
    return jnp.where(i[:, None] == i[None, :], 1.0 / DH_A, 0.0).astype(bf16)


def _layer(l, lp, x, b, seq, rope_tab, states, sl, mem_k, mem_v, ml, mem_tall, paged=None):
    lru_h0, lru_buf, ssm_h0, ssm_buf = states
    lam_init = 0.8 - 0.6 * math.exp(-0.3 * l)
    proj, dt_raw = _in_proj(x, lp['norm_mix'], lp['w_main'], lp['w_dt'])
    cos, sin = rope_tab
    act = f32 if paged is not None else bf16
    q_scale = DH_A ** -0.5 * (1.0 if paged is not None else math.log2(math.e))
    q, k, kb, vb, v = _qk_prep(proj, cos, sin, lp['q_gain'], lp['k_gain'], _head_bd(), act, q_scale)
    if paged is None:
        y_a = _attn_prompt(lp['lambda_qk'], q, kb, vb, lp['subln'], lam_init, b, seq)
    else:
        page_table, cache_kt, cache_v2 = paged
        y_a = _attn_sample(page_table, lp['lambda_qk'], q, kb, vb, lp['subln'], cache_kt, cache_v2, l, lam_init, b, seq)
    y_b, lru_h = _lru(proj, lru_buf, lru_h0, lp['lru_cw'], lp['lru_cb'], lp['lru_wa'], lp['lru_ba'], lp['lru_wx'],
                      lp['lru_bx'], lp['lru_lambda'], sl, b, seq)
    sgu_w, sgu_b = _sgu_weights(lp, seq)
    y_c, vc = _cmix(proj, lp['sgu_norm'], sgu_w, sgu_b)
    y_d, ssm_h = _ssd(proj, dt_raw, ssm_buf, ssm_h0, lp['ssd_cw'], lp['ssd_cb'], lp['ssd_dtb'], lp['ssd_alog'],
                      lp['ssd_d'], lp['ssd_norm'], sl, b, seq)
    x = _merge(x, lp['norm_mix'], y_a, y_b, y_c, y_d, lp['w_gates'], lp['w_branch'], lp['w_o'])
    x = _cross(x, lp['norm_cross'], lp['w_xq'], lp['xq_norm'], mem_k, mem_v, lp['w_xo'], ml, seq, mem_tall)
    x = _ffn(x, lp['norm_ffn'], lp['w_ffn_in'], lp['w_ffn_out'])
    p3 = proj.reshape(b, seq, N_MAIN)
    assert seq >= CONV_W - 1
    new_lru_buf = p3[:, seq - (CONV_W - 1):, COL_XB * 512:(COL_XB + 1) * 512]
    new_ssm_buf = p3[:, seq - (CONV_W - 1):, COL_XBC * 1024:(COL_XBC + 1) * 1024]
    return x, (k, v, lru_h, new_lru_buf, ssm_h, new_ssm_buf, vc)


def kernel(x_prompt, x_sample, mem_prompt, cache_attn_k, cache_attn_v, page_table, cache_mem_k, cache_mem_v, state_lru, state_lru_conv, state_ssm, state_ssm_conv, norm_mix, w_in, q_norm, k_norm, lambda_qk, subln, lru_conv_w, lru_conv_b, lru_wa, lru_ba, lru_wx, lru_bx, lru_lambda, sgu_norm, sgu_w, sgu_b, ssd_conv_w, ssd_conv_b, ssd_dt_bias, ssd_a_log, ssd_d, ssd_norm, w_branch, w_o, norm_cross, w_xq, w_xk, w_xv, xq_norm, xk_norm, w_xo, norm_ffn, w_ffn_in, w_ffn_out):
    P = dict(norm_mix=norm_mix, w_in=w_in, q_norm=q_norm, k_norm=k_norm, lambda_qk=lambda_qk, subln=subln,
             lru_conv_w=lru_conv_w, lru_conv_b=lru_conv_b, lru_wa=lru_wa, lru_ba=lru_ba, lru_wx=lru_wx, lru_bx=lru_bx,
             lru_lambda=lru_lambda, sgu_norm=sgu_norm, sgu_w=sgu_w, sgu_b=sgu_b, ssd_conv_w=ssd_conv_w,
             ssd_conv_b=ssd_conv_b, ssd_dt_bias=ssd_dt_bias, ssd_a_log=ssd_a_log, ssd_d=ssd_d, ssd_norm=ssd_norm,
             w_branch=w_branch, w_o=w_o, norm_cross=norm_cross, w_xq=w_xq, xq_norm=xq_norm, w_xo=w_xo,
             norm_ffn=norm_ffn, w_ffn_in=w_ffn_in, w_ffn_out=w_ffn_out)
    bp, lp_, _ = x_prompt.shape
    bs, ls, _ = x_sample.shape
    n_pages = page_table.shape[1]
    past_len = n_pages * PAGE_SIZE
    n_phys = cache_attn_k.shape[1]

    rope_p = _rope_tables(jnp.arange(lp_))
    cos_s, sin_s = _rope_tables(past_len + jnp.arange(ls))
    rope_s = (jnp.tile(cos_s, (bs, 1)), jnp.tile(sin_s, (bs, 1)))
    cache_kt = jnp.transpose(cache_attn_k, (0, 1, 3, 4, 2))
    cache_v2 = cache_attn_v.reshape(DEPTH, n_phys, PAGE_SIZE * H_A, DV_A)
    mem_k_s = cache_mem_k.reshape(DEPTH, bs, N_MEM * XH, XDH)
    mem_v_s = cache_mem_v.reshape(DEPTH, bs, N_MEM * XH, XDH)

    pmk, pmv, pmk_t, pmv_t = _memkv(mem_prompt.reshape(bp * N_MEM, D_MODEL), w_xk.astype(bf16), w_xv.astype(bf16),
                                    xk_norm[:, None, :])

    yp = x_prompt.reshape(bp * lp_, D_MODEL)
    ys = x_sample.reshape(bs * ls, D_MODEL)
    zeros = lambda *s: jnp.zeros(s, f32)
    states_p = (zeros(1, bp, 1, W_B), zeros(1, bp, CONV_W - 1, W_B), zeros(1, bp, H_D * P_D, N_D),
                zeros(1, bp, CONV_W - 1, XBC_D))
    states_s = (state_lru[:, :, None, :], state_lru_conv, state_ssm.reshape(DEPTH, bs, H_D * P_D, N_D), state_ssm_conv)
    pmk4 = pmk.reshape(DEPTH, bp, N_MEM, XW)
    pmv4 = pmv.reshape(DEPTH, bp, N_MEM, XW)
    p_states, s_states = [], []
    for l in range(DEPTH):
        lp = _layer_params(l, P)
        yp, sp = _layer(l, lp, yp, bp, lp_, rope_p, states_p, 0, pmk4, pmv4, l, False)
        p_states.append(sp)
        ys, ss = _layer(l, lp, ys, bs, ls, rope_s, states_s, l, mem_k_s, mem_v_s, l, True,
                        paged=(page_table, cache_kt, cache_v2))
        s_states.append(ss)

    def stack(states, idx, shape):
        return jnp.stack([s[idx] for s in states]).reshape((DEPTH,) + shape)

    return (
        yp.reshape(bp, lp_, D_MODEL), ys.reshape(bs, ls, D_MODEL),
        stack(p_states, 0, (bp, lp_, 2 * H_A, DH_A)), stack(p_states, 1, (bp, lp_, H_A, DV_A)),
        pmk_t.reshape(DEPTH, bp, N_MEM, XH, XDH), pmv_t.reshape(DEPTH, bp, N_MEM, XH, XDH),
        stack(p_states, 2, (bp, W_B)), stack(p_states, 3, (bp, CONV_W - 1, W_B)),
        stack(p_states, 4, (bp, H_D, P_D, N_D)), stack(p_states, 5, (bp, CONV_W - 1, XBC_D)),
        stack(s_states, 0, (bs, ls, 2 * H_A, DH_A)), stack(s_states, 1, (bs, ls, H_A, DV_A)),
        stack(s_states, 2, (bs, W_B)), stack(s_states, 3, (bs, CONV_W - 1, W_B)),
        stack(s_states, 4, (bs, H_D, P_D, N_D)), stack(s_states, 5, (bs, CONV_W - 1, XBC_D)),
        stack(s_states, 6, (bs, ls, W_C)),
    )
```

```python
import functools
import math

import jax
import jax.numpy as jnp
from jax import lax
from jax.experimental import pallas as pl
from jax.experimental.pallas import tpu as pltpu

f32 = jnp.float32
bf16 = jnp.bfloat16

D_MODEL = 1024
DEPTH = 4
PAGE_SIZE = 128
H_A = 4
DH_A = 64
DV_A = 128
W_A = 512
ROPE_THETA = 10000.0
W_B = 512
NB_B = 8
LRU_C = 8.0
CONV_W = 4
W_C = 512
G_C = 4
CHUNK_C = 128
H_D = 8
P_D = 64
W_D = 512
G_D = 2
N_D = 128
SSD_CHUNK = 128
XBC_D = 1024
N_MEM = 256
XH = 4
XDH = 128
XW = 512
D_FF = 2816
EPS = 1e-6

LANES = 128
SUBLANES = 8
VMEM_LIMIT = 48 * 1024 * 1024

N_BRANCH = 4
N_MAIN = 5120
COL_Q, COL_K, COL_V, COL_XB, COL_GB, COL_UC, COL_VC, COL_ZD = range(8)
COL_XBC = 4


ROW_TILE = 512


def _cparams(sem):
    return pltpu.CompilerParams(dimension_semantics=sem, vmem_limit_bytes=VMEM_LIMIT)


def _resident(shape, index_map):
    return pl.BlockSpec(shape, index_map, pipeline_mode=pl.Buffered(1))


def _dot(a, b):
    return jnp.dot(a, b, preferred_element_type=f32)


def _dot_nt(a, b):
    return lax.dot_general(a, b, (((1,), (1,)), ((), ())), preferred_element_type=f32)


def _dot_tn(a, b):
    return lax.dot_general(a, b, (((0,), (0,)), ((), ())), preferred_element_type=f32)


def _rms(x, g):
    ms = jnp.mean(x * x, axis=-1, keepdims=True)
    return x * lax.rsqrt(ms + EPS) * g


def _softplus(x):
    return jnp.maximum(x, 0.0) + jnp.log1p(jnp.exp(-jnp.abs(x)))


def _in_proj_body(x_ref, g_ref, w_ref, wdt_ref, proj_ref, dt_ref):
    hn = _rms(x_ref[...], g_ref[...]).astype(bf16)
    dt_ref[...] = _dot(hn, wdt_ref[...])
    for c in range(N_MAIN // 512):
        sl = slice(c * 512, (c + 1) * 512)
        acc = _dot(hn, w_ref[:, sl])
        if COL_GB <= c <= COL_VC:
            acc = jax.nn.gelu(acc)
        elif c == COL_ZD:
            acc = acc * jax.nn.sigmoid(acc)
        proj_ref[:, sl] = acc


def _in_proj(x, g, w_main, w_dt):
    t = x.shape[0]
    tm = min(t, ROW_TILE)
    return pl.pallas_call(
        _in_proj_body,
        grid=(t // tm,),
        in_specs=[
            pl.BlockSpec((tm, D_MODEL), lambda i: (i, 0)),
            _resident((1, D_MODEL), lambda i: (0, 0)),
            _resident((D_MODEL, N_MAIN), lambda i: (0, 0)),
            _resident((D_MODEL, LANES), lambda i: (0, 0)),
        ],
        out_specs=[
            pl.BlockSpec((tm, N_MAIN), lambda i: (i, 0)),
            pl.BlockSpec((tm, LANES), lambda i: (i, 0)),
        ],
        out_shape=[jax.ShapeDtypeStruct((t, N_MAIN), f32), jax.ShapeDtypeStruct((t, LANES), f32)],
        compiler_params=_cparams(("parallel",)),
        name="in_proj",
    )(x, g, w_main, w_dt)


def _head_norm_rope(x, gain, cos, sin, bd):
    sq = x * x
    hi = sq.astype(bf16)
    lo = (sq - hi.astype(f32)).astype(bf16)
    ms = _dot(hi, bd) + _dot(lo, bd)
    y = x * lax.rsqrt(ms + EPS) * gain
    lane = lax.broadcasted_iota(jnp.int32, y.shape, 1)
    first_half = (lane % DH_A) < (DH_A // 2)
    swapped = jnp.where(first_half, pltpu.roll(y, LANES - DH_A // 2, 1), pltpu.roll(y, DH_A // 2, 1))
    return y * cos + swapped * sin


def _qk_prep_body(q_ref, k_ref, v_ref, cos_ref, sin_ref, qg_ref, kg_ref, bd_ref, v_all_ref, qo_ref, ko_ref, kb_ref,
                  vb_ref, vt_ref, *, q_scale):
    del v_all_ref
    bd = bd_ref[...]
    rows = v_ref.shape[0]
    for h in range(H_A):
        vt_ref[pl.ds(h, rows, stride=H_A), :] = v_ref[:, h * DV_A:(h + 1) * DV_A]
    for c in range(W_A // LANES):
        sl = slice(c * LANES, (c + 1) * LANES)
        cos = cos_ref[:, sl]
        sin = sin_ref[:, sl]
        q = _head_norm_rope(q_ref[:, sl], qg_ref[...], cos, sin, bd)
        k = _head_norm_rope(k_ref[:, sl], kg_ref[...], cos, sin, bd)
        qo_ref[:, sl] = (q * q_scale).astype(qo_ref.dtype)
        ko_ref[:, sl] = k
        kb_ref[:, sl] = k.astype(kb_ref.dtype)
    vb_ref[...] = v_ref[...].astype(vb_ref.dtype)


def _qk_prep(proj, cos, sin, qg, kg, bd, act_dtype, q_scale, v_all, layer):
    t = proj.shape[0]
    p = cos.shape[0]
    tm = min(t, p, 512)
    npos = p // tm
    row = lambda c: pl.BlockSpec((tm, 512), lambda i, c=c: (i, c))
    tab = pl.BlockSpec((tm, 512), lambda i: (i % npos, 0))
    small = lambda shape: pl.BlockSpec(shape, lambda i: (0, 0))
    return pl.pallas_call(
        functools.partial(_qk_prep_body, q_scale=q_scale),
        grid=(t // tm,),
        in_specs=[row(COL_Q), row(COL_K), row(COL_V), tab, tab, small((1, LANES)), small((1, LANES)),
                  small((LANES, LANES)), pl.BlockSpec(memory_space=pl.ANY)],
        out_specs=[row(0)] * 4 + [pl.BlockSpec((None, tm * H_A, DV_A), lambda i: (layer, i, 0))],
        out_shape=[jax.ShapeDtypeStruct((t, 512), act_dtype), jax.ShapeDtypeStruct((t, 512), f32),
                   jax.ShapeDtypeStruct((t, 512), act_dtype), jax.ShapeDtypeStruct((t, 512), act_dtype),
                   jax.ShapeDtypeStruct(v_all.shape, f32)],
        input_output_aliases={8: 4},
        compiler_params=_cparams(("parallel",)),
        name="qk_prep",
    )(proj, proj, proj, cos, sin, qg, kg, bd, v_all)


def _lambda_value(lq, lam_init):
    a = jnp.sum(lq[0:1] * lq[1:2], axis=-1, keepdims=True)
    b = jnp.sum(lq[2:3] * lq[3:4], axis=-1, keepdims=True)
    return jnp.exp(a) - jnp.exp(b) + lam_init


def _stack_pair(qp):
    lane = lax.broadcasted_iota(jnp.int32, qp.shape, 1)
    zero = jnp.zeros_like(qp)
    return jnp.concatenate([jnp.where(lane < DH_A, qp, zero), jnp.where(lane >= DH_A, qp, zero)], axis=0)


def _diff_finish(acc, l, lam, sub, lam_init, tq):
    o = acc[:tq] / l[:tq] - lam * (acc[tq:] / l[tq:])
    return _rms(o, sub) * (1.0 - lam_init)


def _attn_prompt_body(lq_ref, q_ref, k_ref, v_ref, sub_ref, o_ref, *, lam_init, tq, tk):
    qi = pl.program_id(1)
    lam = _lambda_value(lq_ref[...], lam_init)
    per = tk // tq
    n_full = qi // per
    r = lax.broadcasted_iota(jnp.int32, (2 * tq, tk), 0)
    r = jnp.where(r >= tq, r - tq, r) + (qi % per) * tq
    c = lax.broadcasted_iota(jnp.int32, (2 * tq, tk), 1)
    diag = c <= r
    sls = [slice(h * LANES, (h + 1) * LANES) for h in range(H_A)]
    qqs = [_stack_pair(q_ref[:, sl]) for sl in sls]

    def step(kv, carry, masked):
        rows = pl.ds(pl.multiple_of(kv * tk, tk), tk)
        out = []
        for (m, l, acc), qq, sl in zip(carry, qqs, sls):
            s = _dot_nt(qq, k_ref[rows, sl])
            if masked:
                s = jnp.where(diag, s, -jnp.inf)
            m_new = jnp.maximum(m, jnp.max(s, axis=-1, keepdims=True))
            alpha = jnp.exp2(m - m_new)
            p = jnp.exp2(s - m_new)
            l = alpha * l + jnp.sum(p, axis=-1, keepdims=True)
            acc = alpha * acc + _dot(p.astype(bf16), v_ref[rows, sl])
            out.append((m_new, l, acc))
        return tuple(out)

    init = tuple((jnp.full((2 * tq, 1), -jnp.inf, f32), jnp.zeros((2 * tq, 1), f32), jnp.zeros((2 * tq, LANES), f32))
                 for _ in range(H_A))
    carry = lax.fori_loop(0, n_full, functools.partial(step, masked=False), init)
    carry = step(n_full, carry, True)
    for (_, l, acc), sl in zip(carry, sls):
        o_ref[:, sl] = _diff_finish(acc, l, lam, sub_ref[:, sl], lam_init, tq)


ATTN_TQ = 512
ATTN_TK = 512


def _attn_prompt(lq, q, k, v, sub, lam_init, b, seq):
    tq = min(seq, ATTN_TQ)
    tk = min(seq, ATTN_TK)
    nq = seq // tq
    return pl.pallas_call(
        functools.partial(_attn_prompt_body, lam_init=lam_init, tq=tq, tk=tk),
        grid=(b, nq),
        in_specs=[
            pl.BlockSpec((4, DH_A), lambda i, j: (0, 0)),
            pl.BlockSpec((tq, 512), lambda i, j: (i * nq + j, 0)),
            pl.BlockSpec((seq, 512), lambda i, j: (i, 0)),
            pl.BlockSpec((seq, 512), lambda i, j: (i, 0)),
            pl.BlockSpec((1, 512), lambda i, j: (0, 0)),
        ],
        out_specs=pl.BlockSpec((tq, 512), lambda i, j: (i * nq + j, 0)),
        out_shape=jax.ShapeDtypeStruct((b * seq, 512), f32),
        compiler_params=_cparams(("parallel", "arbitrary")),
        name="attn_prompt",
    )(lq, q, k, v, sub)


def _attn_sample_body(pt_ref, lq_ref, q_ref, kn_ref, vn_ref, sub_ref, *rest, lam_init, n_pages, ls):
    del pt_ref
    k_refs = rest[:n_pages]
    v_refs = rest[n_pages:2 * n_pages]
    o_ref = rest[2 * n_pages]
    lam = _lambda_value(lq_ref[...], lam_init)
    pad = jnp.zeros((PAGE_SIZE - ls, LANES), f32)
    r = lax.broadcasted_iota(jnp.int32, (2 * ls, PAGE_SIZE), 0)
    r = jnp.where(r >= ls, r - ls, r)
    c = lax.broadcasted_iota(jnp.int32, (2 * ls, PAGE_SIZE), 1)
    new_mask = c <= r
    for h in range(H_A):
        sl = slice(h * LANES, (h + 1) * LANES)
        qq = _stack_pair(q_ref[:, sl]).astype(bf16)
        kt = jnp.concatenate([k_refs[p][2 * h:2 * h + 2].reshape(LANES, PAGE_SIZE) for p in range(n_pages)], axis=1)
        s_past = _dot(qq, kt.astype(bf16))
        k_new = jnp.concatenate([kn_ref[:, sl], pad], axis=0).astype(bf16)
        s_new = jnp.where(new_mask, _dot_nt(qq, k_new), -jnp.inf)
        m = jnp.maximum(jnp.max(s_past, axis=-1, keepdims=True), jnp.max(s_new, axis=-1, keepdims=True))
        p_past = jnp.exp(s_past - m)
        p_new = jnp.exp(s_new - m)
        l = jnp.sum(p_past, axis=-1, keepdims=True) + jnp.sum(p_new, axis=-1, keepdims=True)
        v_past = jnp.concatenate([v_refs[p][pl.ds(h, PAGE_SIZE, stride=H_A), :] for p in range(n_pages)], axis=0)
        v_new = jnp.concatenate([vn_ref[:, sl], pad], axis=0).astype(bf16)
        acc = _dot(p_past.astype(bf16), v_past.astype(bf16)) + _dot(p_new.astype(bf16), v_new)
        o_ref[:, sl] = _diff_finish(acc, l, lam, sub_ref[:, sl], lam_init, ls)


def _attn_sample(page_table, lq, q, k_new, v_new, sub, cache_kt, cache_v2, layer, lam_init, b, ls):
    n_pages = page_table.shape[1]
    row = pl.BlockSpec((ls, 512), lambda i, pt: (i, 0))
    k_specs = [pl.BlockSpec((None, None, 2 * H_A, DH_A, PAGE_SIZE), lambda i, pt, p=p: (layer, pt[i, p], 0, 0, 0))
               for p in range(n_pages)]
    v_specs = [pl.BlockSpec((None, None, PAGE_SIZE * H_A, DV_A), lambda i, pt, p=p: (layer, pt[i, p], 0, 0))
               for p in range(n_pages)]
    grid_spec = pltpu.PrefetchScalarGridSpec(
        num_scalar_prefetch=1,
        grid=(b,),
        in_specs=[pl.BlockSpec((4, DH_A), lambda i, pt: (0, 0)), row, row, row,
                  pl.BlockSpec((1, 512), lambda i, pt: (0, 0))] + k_specs + v_specs,
        out_specs=row,
    )
    return pl.pallas_call(
        functools.partial(_attn_sample_body, lam_init=lam_init, n_pages=n_pages, ls=ls),
        grid_spec=grid_spec,
        out_shape=jax.ShapeDtypeStruct((b * ls, 512), f32),
        compiler_params=_cparams(("parallel",)),
        name="attn_sample",
    )(page_table, lq, q, k_new, v_new, sub, *([cache_kt] * n_pages), *([cache_v2] * n_pages))


def _conv_tile(x_ref, buf_ref, w_ref, b_ref, xx_ref, first, tt):
    @pl.when(first)
    def _():
        xx_ref[SUBLANES - (CONV_W - 1):SUBLANES, :] = buf_ref[...]

    @pl.when(jnp.logical_not(first))
    def _():
        xx_ref[SUBLANES - (CONV_W - 1):SUBLANES, :] = xx_ref[tt + SUBLANES - (CONV_W - 1):tt + SUBLANES, :]

    xx_ref[SUBLANES:SUBLANES + tt, :] = x_ref[...]
    y = b_ref[...] + w_ref[0:1, :] * xx_ref[SUBLANES - 3:SUBLANES - 3 + tt, :]
    for j in range(1, CONV_W):
        y = y + w_ref[j:j + 1, :] * xx_ref[SUBLANES - 3 + j:SUBLANES - 3 + j + tt, :]
    return y


def _lru_body(x_ref, gate_ref, buf_ref, h0_ref, cw_ref, cb_ref, wa_ref, ba_ref, wx_ref, bx_ref, lam_ref,
              y_ref, hf_ref, xx_ref, a_ref, u_ref, h_ref, *, n_seq, tt):
    ti = pl.program_id(1)
    xc = [_conv_tile(x_ref.at[pl.ds(s * tt, tt)], buf_ref.at[s], cw_ref, cb_ref, xx_ref.at[s], ti == 0, tt)
          for s in range(n_seq)]
    xc = xc[0] if n_seq == 1 else jnp.concatenate(xc, axis=0)
    xcb = xc.astype(bf16)
    r = jax.nn.sigmoid(_dot(xcb, wa_ref[...]) + ba_ref[...])
    i = jax.nn.sigmoid(_dot(xcb, wx_ref[...]) + bx_ref[...])
    log_a = (-LRU_C) * r * _softplus(-lam_ref[...])
    a = jnp.exp(log_a)
    a_ref[...] = a
    u_ref[...] = jnp.sqrt(1.0 - a * a) * (i * xc)

    @pl.when(ti == 0)
    def _():
        h_ref[...] = h0_ref[...]

    row = lax.broadcasted_iota(jnp.int32, (SUBLANES, W_B), 0)

    def tile(start, h):
        rows = pl.ds(pl.multiple_of(start, SUBLANES), SUBLANES)
        a = a_ref[rows, :]
        u = u_ref[rows, :]
        for sh in (1, 2, 4):
            keep = row >= sh
            u = jnp.where(keep, a * pltpu.roll(u, sh, 0) + u, u)
            a = jnp.where(keep, a * pltpu.roll(a, sh, 0), a)
        hh = a * h + u
        y_ref[rows, :] = hh * gate_ref[rows, :]
        return hh[SUBLANES - 1:SUBLANES, :]

    for s in range(n_seq):
        if tt == SUBLANES:
            h = tile(s * tt, h_ref[s])
        else:
            h = lax.fori_loop(0, tt // SUBLANES, lambda k, h, s=s: tile(s * tt + k * SUBLANES, h), h_ref[s])
        h_ref[s] = h
        hf_ref[s] = h


def _lru(proj, buf, h0, cw, cb, wa, ba, wx, bx, lam, layer, b, seq):
    n_seq, tt = (1, min(seq, 512)) if seq >= LANES else (LANES // seq, seq)
    nt = seq // tt
    rows = n_seq * tt
    small = lambda shape: pl.BlockSpec(shape, lambda i, j: (0,) * len(shape))
    return pl.pallas_call(
        functools.partial(_lru_body, n_seq=n_seq, tt=tt),
        grid=(b // n_seq, nt),
        in_specs=[
            pl.BlockSpec((rows, 512), lambda i, j: (i * nt + j, COL_XB)),
            pl.BlockSpec((rows, 512), lambda i, j: (i * nt + j, COL_GB)),
            pl.BlockSpec((None, n_seq, CONV_W - 1, W_B), lambda i, j: (layer, i, 0, 0)),
            pl.BlockSpec((None, n_seq, 1, W_B), lambda i, j: (layer, i, 0, 0)),
            small((CONV_W, W_B)), small((1, W_B)), small((W_B, W_B)), small((1, W_B)), small((W_B, W_B)),
            small((1, W_B)), small((1, W_B)),
        ],
        out_specs=[
            pl.BlockSpec((rows, 512), lambda i, j: (i * nt + j, 0)),
            pl.BlockSpec((n_seq, 1, W_B), lambda i, j: (i, 0, 0)),
        ],
        out_shape=[jax.ShapeDtypeStruct((b * seq, W_B), f32), jax.ShapeDtypeStruct((b, 1, W_B), f32)],
        scratch_shapes=[pltpu.VMEM((n_seq, tt + SUBLANES, W_B), f32), pltpu.VMEM((rows, W_B), f32),
                        pltpu.VMEM((rows, W_B), f32), pltpu.VMEM((n_seq, 1, W_B), f32)],
        compiler_params=_cparams(("parallel", "arbitrary")),
        name="lru",
    )(proj, proj, buf, h0, cw, cb, wa, ba, wx, bx, lam)


def _cmix_body(u_ref, v_ref, g_ref, w_ref, b_ref, y_ref, *maybe_vn_ref, n_chunks):
    vn = _rms(v_ref[...], g_ref[...])
    for vn_ref in maybe_vn_ref:
        vn_ref[...] = vn
    r = lax.broadcasted_iota(jnp.int32, (CHUNK_C, CHUNK_C), 0)
    c = lax.broadcasted_iota(jnp.int32, (CHUNK_C, CHUNK_C), 1)
    for g in range(G_C):
        sl = slice(g * LANES, (g + 1) * LANES)
        wg = jnp.where(c <= r, w_ref[g], 0.0).astype(bf16)
        bias = b_ref[:, g:g + 1]
        for ch in range(n_chunks):
            rows = slice(ch * CHUNK_C, (ch + 1) * CHUNK_C)
            mixed = _dot(wg, vn[rows, sl].astype(bf16)) + bias
            y_ref[rows, sl] = u_ref[rows, sl] * mixed


def _cmix(proj, g, w, bias, emit_vn):
    t = proj.shape[0]
    tc = min(t, 512)
    n_out = 2 if emit_vn else 1
    return pl.pallas_call(
        functools.partial(_cmix_body, n_chunks=tc // CHUNK_C),
        grid=(t // tc,),
        in_specs=[
            pl.BlockSpec((tc, 512), lambda i: (i, COL_UC)),
            pl.BlockSpec((tc, 512), lambda i: (i, COL_VC)),
            pl.BlockSpec((1, W_C), lambda i: (0, 0)),
            pl.BlockSpec((G_C, CHUNK_C, CHUNK_C), lambda i: (0, 0, 0)),
            pl.BlockSpec((CHUNK_C, G_C), lambda i: (0, 0)),
        ],
        out_specs=[pl.BlockSpec((tc, 512), lambda i: (i, 0))] * n_out,
        out_shape=[jax.ShapeDtypeStruct((t, W_C), f32)] * n_out,
        compiler_params=_cparams(("parallel",)),
        name="cmix",
    )(proj, proj, g, w, bias)


def _ssd_body(xbc_ref, z_ref, dt_ref, buf_ref, h0_ref, cw_ref, cb_ref, dtb_ref, alog_ref, d_ref, g_ref, hf_all_ref,
              y_ref, hf_ref, xx_ref, st_ref, *, n_seq, rows):
    del hf_all_ref
    ci = pl.program_id(1)
    q = SSD_CHUNK

    @pl.when(ci == 0)
    def _():
        st_ref[...] = h0_ref[...]

    xbc = [_conv_tile(xbc_ref.at[pl.ds(s * rows, rows)], buf_ref.at[s], cw_ref, cb_ref, xx_ref.at[s], ci == 0, rows)
           for s in range(n_seq)]
    xbc = xbc[0] if n_seq == 1 else jnp.concatenate(xbc, axis=0)
    xbc = xbc * jax.nn.sigmoid(xbc)
    dtx = dt_ref[...] + dtb_ref[...]
    dt = jnp.maximum(dtx, 0.0) + jnp.log(1.0 + jnp.exp(-jnp.abs(dtx)))
    xs = xbc[:, :W_D]
    a_neg = -jnp.exp(alog_ref[...])
    ri = lax.broadcasted_iota(jnp.int32, (q, q), 0)
    cj = lax.broadcasted_iota(jnp.int32, (q, q), 1)
    causal = cj <= ri
    if n_seq > 1:
        causal = causal & ((ri // rows) == (cj // rows))
    exact_dot = functools.partial(jnp.dot, preferred_element_type=f32, precision=lax.Precision.HIGHEST)
    cum = exact_dot(jnp.where(causal, 1.0, 0.0).astype(f32), dt * a_neg)
    cum_t = cum.T
    if n_seq > 1:
        last_rows = exact_dot(jnp.where(cj == (ri // rows) * rows + (rows - 1), 1.0, 0.0).astype(f32), cum)
    else:
        last_rows = cum[q - 1:q, :]
    lane = lax.broadcasted_iota(jnp.int32, (q, LANES), 1)
    low = lane < P_D
    row_seq = lax.broadcasted_iota(jnp.int32, (q, LANES), 0) // rows
    rlow = lax.broadcasted_iota(jnp.int32, (LANES, N_D), 0) < P_D
    y_parts = []
    for g in range(G_D):
        bm = xbc[:, W_D + g * N_D:W_D + (g + 1) * N_D].astype(bf16)
        cm_f = xbc[:, W_D + (G_D + g) * N_D:W_D + (G_D + g + 1) * N_D]
        cm = cm_f.astype(bf16)
        scores = _dot_nt(cm, bm)
        for k in range(H_D // G_D // 2):
            h0 = g * (H_D // G_D) + 2 * k
            sl = slice(h0 * P_D, (h0 + 2) * P_D)
            col = lambda a, h: a[:, h:h + 1]
            pair = lambda a: jnp.where(low, col(a, h0), col(a, h0 + 1))
            xdt = xs[:, sl] * pair(dt)
            xdt_b = xdt.astype(bf16)
            yd = []
            for h in (h0, h0 + 1):
                seg = col(cum, h) - cum_t[h:h + 1, :]
                m = scores * jnp.exp(jnp.where(causal, seg, -jnp.inf))
                yd.append(_dot(m.astype(bf16), xdt_b))
            xw = xdt * jnp.exp(pair(last_rows - cum))
            offs = []
            for s in range(n_seq):
                rs = slice(s * rows, (s + 1) * rows)
                st = st_ref[s, sl, :]
                offs.append(_dot_nt(cm if n_seq == 1 else cm_f[rs].astype(bf16), st.astype(bf16)))
                last = cum[(s + 1) * rows - 1:(s + 1) * rows, :]
                dec = jnp.exp(jnp.where(rlow, last[:, h0:h0 + 1], last[:, h0 + 1:h0 + 2]))
                xw_s = xw if n_seq == 1 else jnp.where(row_seq == s, xw, 0.0)
                st_ref[s, sl, :] = dec * st + _dot_tn(xw_s.astype(bf16), bm)
            y_off = (offs[0] if n_seq == 1 else jnp.concatenate(offs, axis=0)) * jnp.exp(pair(cum))
            y_parts.append(jnp.where(low, yd[0], yd[1]) + y_off + d_ref[:, sl] * xs[:, sl])
    y = jnp.concatenate(y_parts, axis=1) * z_ref[...]
    y_ref[...] = _rms(y, g_ref[...])
    hf_ref[...] = st_ref[...]


def _ssd(proj, dt_raw, buf, h0, cw, cb, dtb, alog, dvec, g, hf_all, layer, layer_out, b, seq):
    n_seq, rows = (1, SSD_CHUNK) if seq >= SSD_CHUNK else (SSD_CHUNK // seq, seq)
    nc = seq // rows
    q = SSD_CHUNK
    small = lambda shape: pl.BlockSpec(shape, lambda i, j: (0,) * len(shape))
    return pl.pallas_call(
        functools.partial(_ssd_body, n_seq=n_seq, rows=rows),
        grid=(b // n_seq, nc),
        in_specs=[
            pl.BlockSpec((q, XBC_D), lambda i, j: (i * nc + j, COL_XBC)),
            pl.BlockSpec((q, 512), lambda i, j: (i * nc + j, COL_ZD)),
            pl.BlockSpec((q, LANES), lambda i, j: (i * nc + j, 0)),
            pl.BlockSpec((None, n_seq, CONV_W - 1, XBC_D), lambda i, j: (layer, i, 0, 0)),
            pl.BlockSpec((None, n_seq, H_D * P_D, N_D), lambda i, j: (layer, i, 0, 0)),
            small((CONV_W, XBC_D)), small((1, XBC_D)), small((1, LANES)), small((1, LANES)), small((1, W_D)),
            small((1, W_D)), pl.BlockSpec(memory_space=pl.ANY),
        ],
        out_specs=[
            pl.BlockSpec((q, W_D), lambda i, j: (i * nc + j, 0)),
            pl.BlockSpec((None, n_seq, H_D * P_D, N_D), lambda i, j: (layer_out, i, 0, 0)),
        ],
        out_shape=[jax.ShapeDtypeStruct((b * seq, W_D), f32), jax.ShapeDtypeStruct(hf_all.shape, f32)],
        input_output_aliases={11: 1},
        scratch_shapes=[pltpu.VMEM((n_seq, rows + SUBLANES, XBC_D), f32), pltpu.VMEM((n_seq, H_D * P_D, N_D), f32)],
        compiler_params=_cparams(("parallel", "arbitrary")),
        name="ssd",
    )(proj, proj, dt_raw, buf, h0, cw, cb, dtb, alog, dvec, g, hf_all)


def _merge_body(x_ref, g_ref, ya_ref, yb_ref, yc_ref, yd_ref, wg_ref, wb_ref, wo_ref, o_ref):
    x = x_ref[...]
    hn = _rms(x, g_ref[...]).astype(bf16)
    m = None
    for k, y_ref in enumerate((ya_ref, yb_ref, yc_ref, yd_ref)):
        gate = jax.nn.sigmoid(_dot(hn, wg_ref[:, k * D_MODEL:(k + 1) * D_MODEL]))
        t = gate * _dot(y_ref[...].astype(bf16), wb_ref[k])
        m = t if m is None else m + t
    o_ref[...] = x + _dot(m.astype(bf16), wo_ref[...])


def _merge(x, g, ya, yb, yc, yd, wg, wb, wo):
    t = x.shape[0]
    tm = min(t, ROW_TILE)
    row = lambda w: pl.BlockSpec((tm, w), lambda i: (i, 0))
    return pl.pallas_call(
        _merge_body,
        grid=(t // tm,),
        in_specs=[row(D_MODEL), _resident((1, D_MODEL), lambda i: (0, 0)), row(512), row(512), row(512), row(512),
                  _resident((D_MODEL, N_BRANCH * D_MODEL), lambda i: (0, 0)),
                  _resident((N_BRANCH, 512, D_MODEL), lambda i: (0, 0, 0)),
                  _resident((D_MODEL, D_MODEL), lambda i: (0, 0))],
        out_specs=row(D_MODEL),
        out_shape=jax.ShapeDtypeStruct((t, D_MODEL), f32),
        compiler_params=_cparams(("parallel",)),
        name="merge",
    )(x, g, ya, yb, yc, yd, wg, wb, wo)


def _memkv_body(m_ref, wk_ref, wv_ref, g_ref, k_ref, v_ref, kt_ref, vt_ref):
    mb = m_ref[...].astype(bf16)
    rows = mb.shape[0]
    kk = _dot(mb, wk_ref[...])
    vv = _dot(mb, wv_ref[...])
    v_ref[...] = vv
    for h in range(XH):
        sl = slice(h * XDH, (h + 1) * XDH)
        kh = _rms(kk[:, sl], g_ref[...])
        k_ref[:, sl] = kh
        kt_ref[pl.ds(h, rows, stride=XH), :] = kh
        vt_ref[pl.ds(h, rows, stride=XH), :] = vv[:, sl]


def _memkv(mem, wk, wv, g):
    r = mem.shape[0]
    tm = min(r, 512)
    return pl.pallas_call(
        _memkv_body,
        grid=(DEPTH, r // tm),
        in_specs=[pl.BlockSpec((tm, D_MODEL), lambda l, i: (i, 0)),
                  pl.BlockSpec((None, D_MODEL, XW), lambda l, i: (l, 0, 0)),
                  pl.BlockSpec((None, D_MODEL, XW), lambda l, i: (l, 0, 0)),
                  pl.BlockSpec((None, 1, XDH), lambda l, i: (l, 0, 0))],
        out_specs=[pl.BlockSpec((None, tm, XW), lambda l, i: (l, i, 0))] * 2
        + [pl.BlockSpec((None, tm * XH, XDH), lambda l, i: (l, i, 0))] * 2,
        out_shape=[jax.ShapeDtypeStruct((DEPTH, r, XW), f32)] * 2 + [jax.ShapeDtypeStruct((DEPTH, r * XH, XDH), f32)] * 2,
        compiler_params=_cparams(("parallel", "parallel")),
        name="memkv",
    )(mem, wk, wv, g)


def _cross_body(x_ref, g_ref, wq_ref, qg_ref, k_ref, v_ref, wo_ref, o_ref, *, n_seq, rows, tall):
    x = x_ref[...]
    qx = _dot(_rms(x, g_ref[...]).astype(bf16), wq_ref[...])
    qn = [_rms(qx[:, h * XDH:(h + 1) * XDH], qg_ref[...]) for h in range(XH)]

    def attend(sc, v):
        p = jnp.exp(sc - jnp.max(sc, axis=-1, keepdims=True))
        return _dot(p.astype(bf16), v) * (1.0 / jnp.sum(p, axis=-1, keepdims=True))

    if tall:
        r = lax.broadcasted_iota(jnp.int32, (XH * rows, N_MEM * XH), 0)
        c = lax.broadcasted_iota(jnp.int32, (XH * rows, N_MEM * XH), 1)
        same_head = (c % XH) == (r // rows)
        outs = []
        for s in range(n_seq):
            qs = jnp.concatenate([q[s * rows:(s + 1) * rows] for q in qn], axis=0)
            sc = _dot_nt(qs.astype(bf16), k_ref[s].astype(bf16)) * (XDH ** -0.5)
            o = attend(jnp.where(same_head, sc, -jnp.inf), v_ref[s].astype(bf16))
            outs.append(jnp.concatenate([o[h * rows:(h + 1) * rows] for h in range(XH)], axis=1))
        ox = jnp.concatenate(outs, axis=0)
    else:
        heads = []
        for h in range(XH):
            sl = slice(h * XDH, (h + 1) * XDH)
            outs = []
            for s in range(n_seq):
                sc = _dot_nt(qn[h][s * rows:(s + 1) * rows].astype(bf16), k_ref[s, :, sl].astype(bf16)) * (XDH ** -0.5)
                outs.append(attend(sc, v_ref[s, :, sl].astype(bf16)))
            heads.append(outs[0] if n_seq == 1 else jnp.concatenate(outs, axis=0))
        ox = jnp.concatenate(heads, axis=1)
    o_ref[...] = x + _dot(ox.astype(bf16), wo_ref[...])


def _cross(x, g, wq, qg, mem_k, mem_v, wo, layer, seq, tall):
    t = x.shape[0]
    if seq >= 512:
        n_seq, rows = 1, 512
        per = seq // rows
        mem_idx = lambda i: (layer, i // per, 0, 0)
    else:
        n_seq, rows = 8, seq
        mem_idx = lambda i: (layer, i, 0, 0)
    tm = n_seq * rows
    small = lambda shape: pl.BlockSpec(shape, lambda i: (0,) * len(shape))
    mem_block = (None, n_seq) + tuple(mem_k.shape[2:])
    return pl.pallas_call(
        functools.partial(_cross_body, n_seq=n_seq, rows=rows, tall=tall),
        grid=(t // tm,),
        in_specs=[pl.BlockSpec((tm, D_MODEL), lambda i: (i, 0)), small((1, D_MODEL)), small((D_MODEL, XW)),
                  small((1, XDH)), pl.BlockSpec(mem_block, mem_idx), pl.BlockSpec(mem_block, mem_idx),
                  small((XW, D_MODEL))],
        out_specs=pl.BlockSpec((tm, D_MODEL), lambda i: (i, 0)),
        out_shape=jax.ShapeDtypeStruct((t, D_MODEL), f32),
        compiler_params=_cparams(("parallel",)),
        name="cross",
    )(x, g, wq, qg, mem_k, mem_v, wo)


def _ffn_body(x_ref, g_ref, wg_ref, wu_ref, wo_ref, o_ref):
    x = x_ref[...]
    hn = _rms(x, g_ref[...]).astype(bf16)
    gate = _dot(hn, wg_ref[...])
    up = _dot(hn, wu_ref[...])
    act = (gate * jax.nn.sigmoid(gate) * up).astype(bf16)
    o_ref[...] = x + _dot(act, wo_ref[...])


def _ffn(x, g, w_in, w_out):
    t = x.shape[0]
    tm = min(t, ROW_TILE)
    return pl.pallas_call(
        _ffn_body,
        grid=(t // tm,),
        in_specs=[pl.BlockSpec((tm, D_MODEL), lambda i: (i, 0)),
                  _resident((1, D_MODEL), lambda i: (0, 0)),
                  _resident((D_MODEL, D_FF), lambda i: (0, 0)),
                  _resident((D_MODEL, D_FF), lambda i: (0, 1)),
                  _resident((D_FF, D_MODEL), lambda i: (0, 0))],
        out_specs=pl.BlockSpec((tm, D_MODEL), lambda i: (i, 0)),
        out_shape=jax.ShapeDtypeStruct((t, D_MODEL), f32),
        compiler_params=_cparams(("parallel",)),
        name="ffn",
    )(x, g, w_in, w_in, w_out)


def _rope_tables(pos):
    half = DH_A // 2
    inv = ROPE_THETA ** (-jnp.arange(half, dtype=f32) / half)
    ang = pos.astype(f32)[:, None] * inv[None, :]
    cos, sin = jnp.cos(ang), jnp.sin(ang)
    cos_h = jnp.concatenate([cos, cos], axis=-1)
    sin_h = jnp.concatenate([-sin, sin], axis=-1)
    reps = W_A // DH_A
    return jnp.tile(cos_h, (1, reps)), jnp.tile(sin_h, (1, reps))


def _block_diag(w):
    nb, a, _ = w.shape
    eye = jnp.eye(nb, dtype=w.dtype)
    return (eye[:, None, :, None] * w[:, :, None, :]).reshape(nb * a, nb * a)


def _layer_params(l, P):
    w_in = P['w_in'][l]
    split_dt = 8 * 512 + XBC_D
    lp = {
        'norm_mix': P['norm_mix'][l][None],
        'w_main': w_in[:, :split_dt].astype(bf16),
        'w_gates': w_in[:, split_dt + H_D:].astype(bf16),
        'w_dt': jnp.pad(w_in[:, split_dt:split_dt + H_D], ((0, 0), (0, LANES - H_D))).astype(bf16),
        'q_gain': jnp.tile(P['q_norm'][l], 2)[None],
        'k_gain': jnp.tile(P['k_norm'][l], 2)[None],
        'lambda_qk': P['lambda_qk'][l],
        'subln': P['subln'][l].reshape(1, W_A),
        'lru_cw': P['lru_conv_w'][l], 'lru_cb': P['lru_conv_b'][l][None],
        'lru_wa': _block_diag(P['lru_wa'][l]).astype(bf16), 'lru_ba': P['lru_ba'][l][None],
        'lru_wx': _block_diag(P['lru_wx'][l]).astype(bf16), 'lru_bx': P['lru_bx'][l][None],
        'lru_lambda': P['lru_lambda'][l][None],
        'sgu_norm': P['sgu_norm'][l][None],
        'sgu_w': P['sgu_w'][l], 'sgu_b': P['sgu_b'][l],
        'ssd_cw': P['ssd_conv_w'][l], 'ssd_cb': P['ssd_conv_b'][l][None],
        'ssd_dtb': jnp.pad(P['ssd_dt_bias'][l], (0, LANES - H_D))[None],
        'ssd_alog': jnp.pad(P['ssd_a_log'][l], (0, LANES - H_D))[None],
        'ssd_d': jnp.repeat(P['ssd_d'][l], P_D)[None],
        'ssd_norm': P['ssd_norm'][l][None],
        'w_branch': P['w_branch'][l].astype(bf16), 'w_o': P['w_o'][l].astype(bf16),
        'norm_cross': P['norm_cross'][l][None], 'w_xq': P['w_xq'][l].astype(bf16),
        'xq_norm': P['xq_norm'][l][None], 'w_xo': P['w_xo'][l].astype(bf16),
        'norm_ffn': P['norm_ffn'][l][None], 'w_ffn_in': P['w_ffn_in'][l].astype(bf16),
        'w_ffn_out': P['w_ffn_out'][l].astype(bf16),
    }
    return lp


def _sgu_weights(lp, seq):
    w, bias = lp['sgu_w'], lp['sgu_b']
    if seq >= CHUNK_C:
        return w, bias.T
    reps = CHUNK_C // seq
    eye = jnp.eye(reps, dtype=w.dtype)
    wk = (eye[None, :, None, :, None] * w[:, None, :seq, None, :seq]).reshape(G_C, CHUNK_C, CHUNK_C)
    return wk, jnp.tile(bias[:, :seq], (1, reps)).T


def _head_bd():
    i = jnp.arange(LANES) // DH_A
    return jnp.where(i[:, None] == i[None, :], 1.0 / DH_A, 0.0).astype(bf16)


def _layer(l, lp, x, b, seq, rope_tab, states, sl, mem_k, mem_v, ml, mem_tall, stacked, paged=None):
    lru_h0, lru_buf, ssm_h0, ssm_buf = states
    v_all, ssm_all = stacked
    lam_init = 0.8 - 0.6 * math.exp(-0.3 * l)
    proj, dt_raw = _in_proj(x, lp['norm_mix'], lp['w_main'], lp['w_dt'])
    cos, sin = rope_tab
    act = f32 if paged is not None else bf16
    q_scale = DH_A ** -0.5 * (1.0 if paged is not None else math.log2(math.e))
    q, k, kb, vb, v_all = _qk_prep(proj, cos, sin, lp['q_gain'], lp['k_gain'], _head_bd(), act, q_scale, v_all, l)
    if paged is None:
        y_a = _attn_prompt(lp['lambda_qk'], q, kb, vb, lp['subln'], lam_init, b, seq)
    else:
        page_table, cache_kt, cache_v2 = paged
        y_a = _attn_sample(page_table, lp['lambda_qk'], q, kb, vb, lp['subln'], cache_kt, cache_v2, l, lam_init, b, seq)
    y_b, lru_h = _lru(proj, lru_buf, lru_h0, lp['lru_cw'], lp['lru_cb'], lp['lru_wa'], lp['lru_ba'], lp['lru_wx'],
                      lp['lru_bx'], lp['lru_lambda'], sl, b, seq)
    sgu_w, sgu_b = _sgu_weights(lp, seq)
    y_c, *vc = _cmix(proj, lp['sgu_norm'], sgu_w, sgu_b, emit_vn=paged is not None)
    vc = vc[0] if vc else None
    y_d, ssm_all = _ssd(proj, dt_raw, ssm_buf, ssm_h0, lp['ssd_cw'], lp['ssd_cb'], lp['ssd_dtb'], lp['ssd_alog'],
                        lp['ssd_d'], lp['ssd_norm'], ssm_all, sl, l, b, seq)
    x = _merge(x, lp['norm_mix'], y_a, y_b, y_c, y_d, lp['w_gates'], lp['w_branch'], lp['w_o'])
    x = _cross(x, lp['norm_cross'], lp['w_xq'], lp['xq_norm'], mem_k, mem_v, lp['w_xo'], ml, seq, mem_tall)
    x = _ffn(x, lp['norm_ffn'], lp['w_ffn_in'], lp['w_ffn_out'])
    p3 = proj.reshape(b, seq, N_MAIN)
    assert seq >= CONV_W - 1
    new_lru_buf = p3[:, seq - (CONV_W - 1):, COL_XB * 512:(COL_XB + 1) * 512]
    new_ssm_buf = p3[:, seq - (CONV_W - 1):, COL_XBC * 1024:(COL_XBC + 1) * 1024]
    return x, (k, lru_h, new_lru_buf, new_ssm_buf, vc), (v_all, ssm_all)


def kernel(x_prompt, x_sample, mem_prompt, cache_attn_k, cache_attn_v, page_table, cache_mem_k, cache_mem_v, state_lru, state_lru_conv, state_ssm, state_ssm_conv, norm_mix, w_in, q_norm, k_norm, lambda_qk, subln, lru_conv_w, lru_conv_b, lru_wa, lru_ba, lru_wx, lru_bx, lru_lambda, sgu_norm, sgu_w, sgu_b, ssd_conv_w, ssd_conv_b, ssd_dt_bias, ssd_a_log, ssd_d, ssd_norm, w_branch, w_o, norm_cross, w_xq, w_xk, w_xv, xq_norm, xk_norm, w_xo, norm_ffn, w_ffn_in, w_ffn_out):
    P = dict(norm_mix=norm_mix, w_in=w_in, q_norm=q_norm, k_norm=k_norm, lambda_qk=lambda_qk, subln=subln,
             lru_conv_w=lru_conv_w, lru_conv_b=lru_conv_b, lru_wa=lru_wa, lru_ba=lru_ba, lru_wx=lru_wx, lru_bx=lru_bx,
             lru_lambda=lru_lambda, sgu_norm=sgu_norm, sgu_w=sgu_w, sgu_b=sgu_b, ssd_conv_w=ssd_conv_w,
             ssd_conv_b=ssd_conv_b, ssd_dt_bias=ssd_dt_bias, ssd_a_log=ssd_a_log, ssd_d=ssd_d, ssd_norm=ssd_norm,
             w_branch=w_branch, w_o=w_o, norm_cross=norm_cross, w_xq=w_xq, xq_norm=xq_norm, w_xo=w_xo,
             norm_ffn=norm_ffn, w_ffn_in=w_ffn_in, w_ffn_out=w_ffn_out)
    bp, lp_, _ = x_prompt.shape
    bs, ls, _ = x_sample.shape
    n_pages = page_table.shape[1]
    past_len = n_pages * PAGE_SIZE
    n_phys = cache_attn_k.shape[1]

    rope_p = _rope_tables(jnp.arange(lp_))
    cos_s, sin_s = _rope_tables(past_len + jnp.arange(ls))
    rope_s = (jnp.tile(cos_s, (bs, 1)), jnp.tile(sin_s, (bs, 1)))
    cache_kt = jnp.transpose(cache_attn_k, (0, 1, 3, 4, 2))
    cache_v2 = cache_attn_v.reshape(DEPTH, n_phys, PAGE_SIZE * H_A, DV_A)
    mem_k_s = cache_mem_k.reshape(DEPTH, bs, N_MEM * XH, XDH)
    mem_v_s = cache_mem_v.reshape(DEPTH, bs, N_MEM * XH, XDH)

    pmk, pmv, pmk_t, pmv_t = _memkv(mem_prompt.reshape(bp * N_MEM, D_MODEL), w_xk.astype(bf16), w_xv.astype(bf16),
                                    xk_norm[:, None, :])

    yp = x_prompt.reshape(bp * lp_, D_MODEL)
    ys = x_sample.reshape(bs * ls, D_MODEL)
    zeros = lambda *s: jnp.zeros(s, f32)
    states_p = (zeros(1, bp, 1, W_B), zeros(1, bp, CONV_W - 1, W_B), zeros(1, bp, H_D * P_D, N_D),
                zeros(1, bp, CONV_W - 1, XBC_D))
    states_s = (state_lru[:, :, None, :], state_lru_conv, state_ssm.reshape(DEPTH, bs, H_D * P_D, N_D), state_ssm_conv)
    pmk4 = pmk.reshape(DEPTH, bp, N_MEM, XW)
    pmv4 = pmv.reshape(DEPTH, bp, N_MEM, XW)
    stacked_p = (zeros(DEPTH, bp * lp_ * H_A, DV_A), zeros(DEPTH, bp, H_D * P_D, N_D))
    stacked_s = (zeros(DEPTH, bs * ls * H_A, DV_A), zeros(DEPTH, bs, H_D * P_D, N_D))
    p_states, s_states = [], []
    for l in range(DEPTH):
        lp = _layer_params(l, P)
        yp, sp, stacked_p = _layer(l, lp, yp, bp, lp_, rope_p, states_p, 0, pmk4, pmv4, l, False, stacked_p)
        p_states.append(sp)
        ys, ss, stacked_s = _layer(l, lp, ys, bs, ls, rope_s, states_s, l, mem_k_s, mem_v_s, l, True, stacked_s,
                                   paged=(page_table, cache_kt, cache_v2))
        s_states.append(ss)

    def stack(states, idx, shape):
        return jnp.stack([s[idx] for s in states]).reshape((DEPTH,) + shape)

    return (
        yp.reshape(bp, lp_, D_MODEL), ys.reshape(bs, ls, D_MODEL),
        stack(p_states, 0, (bp, lp_, 2 * H_A, DH_A)), stacked_p[0].reshape(DEPTH, bp, lp_, H_A, DV_A),
        pmk_t.reshape(DEPTH, bp, N_MEM, XH, XDH), pmv_t.reshape(DEPTH, bp, N_MEM, XH, XDH),
        stack(p_states, 1, (bp, W_B)), stack(p_states, 2, (bp, CONV_W - 1, W_B)),
        stacked_p[1].reshape(DEPTH, bp, H_D, P_D, N_D), stack(p_states, 3, (bp, CONV_W - 1, XBC_D)),
        stack(s_states, 0, (bs, ls, 2 * H_A, DH_A)), stacked_s[0].reshape(DEPTH, bs, ls, H_A, DV_A),
        stack(s_states, 1, (bs, W_B)), stack(s_states, 2, (bs, CONV_W - 1, W_B)),
        stacked_s[1].reshape(DEPTH, bs, H_D, P_D, N_D), stack(s_states, 3, (bs, CONV_W - 1, XBC_D)),
        stack(s_states, 4, (bs, ls, W_C)),
    )
```

```python
import functools
import math

import jax
import jax.numpy as jnp
from jax import lax
from jax.experimental import pallas as pl
from jax.experimental.pallas import tpu as pltpu

f32 = jnp.float32
bf16 = jnp.bfloat16

D_MODEL = 1024
DEPTH = 4
PAGE_SIZE = 128
H_A = 4
DH_A = 64
DV_A = 128
W_A = 512
ROPE_THETA = 10000.0
W_B = 512
NB_B = 8
LRU_C = 8.0
CONV_W = 4
W_C = 512
G_C = 4
CHUNK_C = 128
H_D = 8
P_D = 64
W_D = 512
G_D = 2
N_D = 128
SSD_CHUNK = 128
XBC_D = 1024
N_MEM = 256
XH = 4
XDH = 128
XW = 512
D_FF = 2816
EPS = 1e-6

LANES = 128
SUBLANES = 8
VMEM_LIMIT = 48 * 1024 * 1024

N_BRANCH = 4
N_MAIN = 5120
COL_Q, COL_K, COL_V, COL_XB, COL_GB, COL_UC, COL_VC, COL_ZD = range(8)
COL_XBC = 4


ROW_TILE = 512


def _cparams(sem):
    return pltpu.CompilerParams(dimension_semantics=sem, vmem_limit_bytes=VMEM_LIMIT)


def _resident(shape, index_map):
    return pl.BlockSpec(shape, index_map, pipeline_mode=pl.Buffered(1))


def _dot(a, b):
    return jnp.dot(a, b, preferred_element_type=f32)


def _dot_nt(a, b):
    return lax.dot_general(a, b, (((1,), (1,)), ((), ())), preferred_element_type=f32)


def _dot_tn(a, b):
    return lax.dot_general(a, b, (((0,), (0,)), ((), ())), preferred_element_type=f32)


def _rms(x, g):
    ms = jnp.mean(x * x, axis=-1, keepdims=True)
    return x * lax.rsqrt(ms + EPS) * g


def _softplus(x):
    return jnp.maximum(x, 0.0) + jnp.log1p(jnp.exp(-jnp.abs(x)))


def _in_proj_body(x_ref, g_ref, w_ref, wdt_ref, proj_ref, dt_ref):
    hn = _rms(x_ref[...], g_ref[...]).astype(bf16)
    dt_ref[...] = _dot(hn, wdt_ref[...])
    for c in range(N_MAIN // 512):
        sl = slice(c * 512, (c + 1) * 512)
        acc = _dot(hn, w_ref[:, sl])
        if COL_GB <= c <= COL_VC:
            acc = jax.nn.gelu(acc)
        elif c == COL_ZD:
            acc = acc * jax.nn.sigmoid(acc)
        proj_ref[:, sl] = acc


def _in_proj(x, g, w_main, w_dt):
    t = x.shape[0]
    tm = min(t, ROW_TILE)
    return pl.pallas_call(
        _in_proj_body,
        grid=(t // tm,),
        in_specs=[
            pl.BlockSpec((tm, D_MODEL), lambda i: (i, 0)),
            _resident((1, D_MODEL), lambda i: (0, 0)),
            _resident((D_MODEL, N_MAIN), lambda i: (0, 0)),
            _resident((D_MODEL, LANES), lambda i: (0, 0)),
        ],
        out_specs=[
            pl.BlockSpec((tm, N_MAIN), lambda i: (i, 0)),
            pl.BlockSpec((tm, LANES), lambda i: (i, 0)),
        ],
        out_shape=[jax.ShapeDtypeStruct((t, N_MAIN), f32), jax.ShapeDtypeStruct((t, LANES), f32)],
        compiler_params=_cparams(("parallel",)),
        name="in_proj",
    )(x, g, w_main, w_dt)


def _head_norm_rope(x, gain, cos, sin, bd):
    sq = x * x
    hi = sq.astype(bf16)
    lo = (sq - hi.astype(f32)).astype(bf16)
    ms = _dot(hi, bd) + _dot(lo, bd)
    y = x * lax.rsqrt(ms + EPS) * gain
    lane = lax.broadcasted_iota(jnp.int32, y.shape, 1)
    first_half = (lane % DH_A) < (DH_A // 2)
    swapped = jnp.where(first_half, pltpu.roll(y, LANES - DH_A // 2, 1), pltpu.roll(y, DH_A // 2, 1))
    return y * cos + swapped * sin


def _qk_prep_body(q_ref, k_ref, v_ref, cos_ref, sin_ref, qg_ref, kg_ref, bd_ref, v_all_ref, qo_ref, ko_ref, kb_ref,
                  vb_ref, vt_ref, *, q_scale):
    del v_all_ref
    bd = bd_ref[...]
    rows = v_ref.shape[0]
    for h in range(H_A):
        vt_ref[pl.ds(h, rows, stride=H_A), :] = v_ref[:, h * DV_A:(h + 1) * DV_A]
    for c in range(W_A // LANES):
        sl = slice(c * LANES, (c + 1) * LANES)
        cos = cos_ref[:, sl]
        sin = sin_ref[:, sl]
        q = _head_norm_rope(q_ref[:, sl], qg_ref[...], cos, sin, bd)
        k = _head_norm_rope(k_ref[:, sl], kg_ref[...], cos, sin, bd)
        qo_ref[:, sl] = (q * q_scale).astype(qo_ref.dtype)
        ko_ref[:, sl] = k
        kb_ref[:, sl] = k.astype(kb_ref.dtype)
    vb_ref[...] = v_ref[...].astype(vb_ref.dtype)


def _qk_prep(proj, cos, sin, qg, kg, bd, act_dtype, q_scale, v_all, layer):
    t = proj.shape[0]
    p = cos.shape[0]
    tm = min(t, p, 512)
    npos = p // tm
    row = lambda c: pl.BlockSpec((tm, 512), lambda i, c=c: (i, c))
    tab = pl.BlockSpec((tm, 512), lambda i: (i % npos, 0))
    small = lambda shape: pl.BlockSpec(shape, lambda i: (0, 0))
    return pl.pallas_call(
        functools.partial(_qk_prep_body, q_scale=q_scale),
        grid=(t // tm,),
        in_specs=[row(COL_Q), row(COL_K), row(COL_V), tab, tab, small((1, LANES)), small((1, LANES)),
                  small((LANES, LANES)), pl.BlockSpec(memory_space=pl.ANY)],
        out_specs=[row(0)] * 4 + [pl.BlockSpec((None, tm * H_A, DV_A), lambda i: (layer, i, 0))],
        out_shape=[jax.ShapeDtypeStruct((t, 512), act_dtype), jax.ShapeDtypeStruct((t, 512), f32),
                   jax.ShapeDtypeStruct((t, 512), act_dtype), jax.ShapeDtypeStruct((t, 512), act_dtype),
                   jax.ShapeDtypeStruct(v_all.shape, f32)],
        input_output_aliases={8: 4},
        compiler_params=_cparams(("parallel",)),
        name="qk_prep",
    )(proj, proj, proj, cos, sin, qg, kg, bd, v_all)


def _lambda_value(lq, lam_init):
    a = jnp.sum(lq[0:1] * lq[1:2], axis=-1, keepdims=True)
    b = jnp.sum(lq[2:3] * lq[3:4], axis=-1, keepdims=True)
    return jnp.exp(a) - jnp.exp(b) + lam_init


def _stack_pair(qp):
    lane = lax.broadcasted_iota(jnp.int32, qp.shape, 1)
    zero = jnp.zeros_like(qp)
    return jnp.concatenate([jnp.where(lane < DH_A, qp, zero), jnp.where(lane >= DH_A, qp, zero)], axis=0)


def _diff_finish(acc, l, lam, sub, lam_init, tq):
    o = acc[:tq] / l[:tq] - lam * (acc[tq:] / l[tq:])
    return _rms(o, sub) * (1.0 - lam_init)


def _attn_prompt_body(lq_ref, q_ref, k_ref, v_ref, sub_ref, o_ref, *, lam_init, tq, tk):
    qi = pl.program_id(1)
    lam = _lambda_value(lq_ref[...], lam_init)
    per = tk // tq
    n_full = qi // per
    r = lax.broadcasted_iota(jnp.int32, (2 * tq, tk), 0)
    r = jnp.where(r >= tq, r - tq, r) + (qi % per) * tq
    c = lax.broadcasted_iota(jnp.int32, (2 * tq, tk), 1)
    diag = c <= r
    sls = [slice(h * LANES, (h + 1) * LANES) for h in range(H_A)]
    qqs = [_stack_pair(q_ref[:, sl]) for sl in sls]

    def step(kv, carry, masked):
        rows = pl.ds(pl.multiple_of(kv * tk, tk), tk)
        out = []
        for (m, l, acc), qq, sl in zip(carry, qqs, sls):
            s = _dot_nt(qq, k_ref[rows, sl])
            if masked:
                s = jnp.where(diag, s, -jnp.inf)
            m_new = jnp.maximum(m, jnp.max(s, axis=-1, keepdims=True))
            alpha = jnp.exp2(m - m_new)
            p = jnp.exp2(s - m_new)
            l = alpha * l + jnp.sum(p, axis=-1, keepdims=True)
            acc = alpha * acc + _dot(p.astype(bf16), v_ref[rows, sl])
            out.append((m_new, l, acc))
        return tuple(out)

    init = tuple((jnp.full((2 * tq, 1), -jnp.inf, f32), jnp.zeros((2 * tq, 1), f32), jnp.zeros((2 * tq, LANES), f32))
                 for _ in range(H_A))
    carry = lax.fori_loop(0, n_full, functools.partial(step, masked=False), init)
    carry = step(n_full, carry, True)
    for (_, l, acc), sl in zip(carry, sls):
        o_ref[:, sl] = _diff_finish(acc, l, lam, sub_ref[:, sl], lam_init, tq)


ATTN_TQ = 512
ATTN_TK = 512


def _attn_prompt(lq, q, k, v, sub, lam_init, b, seq):
    tq = min(seq, ATTN_TQ)
    tk = min(seq, ATTN_TK)
    nq = seq // tq
    return pl.pallas_call(
        functools.partial(_attn_prompt_body, lam_init=lam_init, tq=tq, tk=tk),
        grid=(b, nq),
        in_specs=[
            pl.BlockSpec((4, DH_A), lambda i, j: (0, 0)),
            pl.BlockSpec((tq, 512), lambda i, j: (i * nq + j, 0)),
            pl.BlockSpec((seq, 512), lambda i, j: (i, 0)),
            pl.BlockSpec((seq, 512), lambda i, j: (i, 0)),
            pl.BlockSpec((1, 512), lambda i, j: (0, 0)),
        ],
        out_specs=pl.BlockSpec((tq, 512), lambda i, j: (i * nq + j, 0)),
        out_shape=jax.ShapeDtypeStruct((b * seq, 512), f32),
        compiler_params=_cparams(("parallel", "arbitrary")),
        name="attn_prompt",
    )(lq, q, k, v, sub)


def _attn_sample_body(pt_ref, lq_ref, q_ref, kn_ref, vn_ref, sub_ref, *rest, lam_init, n_pages, ls):
    del pt_ref
    k_refs = rest[:n_pages]
    v_refs = rest[n_pages:2 * n_pages]
    o_ref = rest[2 * n_pages]
    lam = _lambda_value(lq_ref[...], lam_init)
    pad = jnp.zeros((PAGE_SIZE - ls, LANES), f32)
    r = lax.broadcasted_iota(jnp.int32, (2 * ls, PAGE_SIZE), 0)
    r = jnp.where(r >= ls, r - ls, r)
    c = lax.broadcasted_iota(jnp.int32, (2 * ls, PAGE_SIZE), 1)
    new_mask = c <= r
    for h in range(H_A):
        sl = slice(h * LANES, (h + 1) * LANES)
        qq = _stack_pair(q_ref[:, sl]).astype(bf16)
        kt = jnp.concatenate([k_refs[p][2 * h:2 * h + 2].reshape(LANES, PAGE_SIZE) for p in range(n_pages)], axis=1)
        s_past = _dot(qq, kt.astype(bf16))
        k_new = jnp.concatenate([kn_ref[:, sl], pad], axis=0).astype(bf16)
        s_new = jnp.where(new_mask, _dot_nt(qq, k_new), -jnp.inf)
        m = jnp.maximum(jnp.max(s_past, axis=-1, keepdims=True), jnp.max(s_new, axis=-1, keepdims=True))
        p_past = jnp.exp(s_past - m)
        p_new = jnp.exp(s_new - m)
        l = jnp.sum(p_past, axis=-1, keepdims=True) + jnp.sum(p_new, axis=-1, keepdims=True)
        v_past = jnp.concatenate([v_refs[p][pl.ds(h, PAGE_SIZE, stride=H_A), :] for p in range(n_pages)], axis=0)
        v_new = jnp.concatenate([vn_ref[:, sl], pad], axis=0).astype(bf16)
        acc = _dot(p_past.astype(bf16), v_past.astype(bf16)) + _dot(p_new.astype(bf16), v_new)
        o_ref[:, sl] = _diff_finish(acc, l, lam, sub_ref[:, sl], lam_init, ls)


def _attn_sample(page_table, lq, q, k_new, v_new, sub, cache_kt, cache_v2, layer, lam_init, b, ls):
    n_pages = page_table.shape[1]
    row = pl.BlockSpec((ls, 512), lambda i, pt: (i, 0))
    k_specs = [pl.BlockSpec((None, None, 2 * H_A, DH_A, PAGE_SIZE), lambda i, pt, p=p: (layer, pt[i, p], 0, 0, 0))
               for p in range(n_pages)]
    v_specs = [pl.BlockSpec((None, None, PAGE_SIZE * H_A, DV_A), lambda i, pt, p=p: (layer, pt[i, p], 0, 0))
               for p in range(n_pages)]
    grid_spec = pltpu.PrefetchScalarGridSpec(
        num_scalar_prefetch=1,
        grid=(b,),
        in_specs=[pl.BlockSpec((4, DH_A), lambda i, pt: (0, 0)), row, row, row,
                  pl.BlockSpec((1, 512), lambda i, pt: (0, 0))] + k_specs + v_specs,
        out_specs=row,
    )
    return pl.pallas_call(
        functools.partial(_attn_sample_body, lam_init=lam_init, n_pages=n_pages, ls=ls),
        grid_spec=grid_spec,
        out_shape=jax.ShapeDtypeStruct((b * ls, 512), f32),
        compiler_params=_cparams(("parallel",)),
        name="attn_sample",
    )(page_table, lq, q, k_new, v_new, sub, *([cache_kt] * n_pages), *([cache_v2] * n_pages))


def _conv_tile(x_ref, buf_ref, w_ref, b_ref, xx_ref, first, tt):
    @pl.when(first)
    def _():
        xx_ref[SUBLANES - (CONV_W - 1):SUBLANES, :] = buf_ref[...]

    @pl.when(jnp.logical_not(first))
    def _():
        xx_ref[SUBLANES - (CONV_W - 1):SUBLANES, :] = xx_ref[tt + SUBLANES - (CONV_W - 1):tt + SUBLANES, :]

    xx_ref[SUBLANES:SUBLANES + tt, :] = x_ref[...]
    y = b_ref[...] + w_ref[0:1, :] * xx_ref[SUBLANES - 3:SUBLANES - 3 + tt, :]
    for j in range(1, CONV_W):
        y = y + w_ref[j:j + 1, :] * xx_ref[SUBLANES - 3 + j:SUBLANES - 3 + j + tt, :]
    return y


def _lru_body(x_ref, gate_ref, buf_ref, h0_ref, cw_ref, cb_ref, wa_ref, ba_ref, wx_ref, bx_ref, lam_ref,
              y_ref, hf_ref, xx_ref, a_ref, u_ref, h_ref, *, n_seq, tt):
    ti = pl.program_id(1)
    xc = [_conv_tile(x_ref.at[pl.ds(s * tt, tt)], buf_ref.at[s], cw_ref, cb_ref, xx_ref.at[s], ti == 0, tt)
          for s in range(n_seq)]
    xc = xc[0] if n_seq == 1 else jnp.concatenate(xc, axis=0)
    xcb = xc.astype(bf16)
    r = jax.nn.sigmoid(_dot(xcb, wa_ref[...]) + ba_ref[...])
    i = jax.nn.sigmoid(_dot(xcb, wx_ref[...]) + bx_ref[...])
    log_a = (-LRU_C) * r * _softplus(-lam_ref[...])
    a = jnp.exp(log_a)
    a_ref[...] = a
    u_ref[...] = jnp.sqrt(1.0 - a * a) * (i * xc)

    @pl.when(ti == 0)
    def _():
        h_ref[...] = h0_ref[...]

    row = lax.broadcasted_iota(jnp.int32, (SUBLANES, W_B), 0)

    def tile(start, h):
        rows = pl.ds(pl.multiple_of(start, SUBLANES), SUBLANES)
        a = a_ref[rows, :]
        u = u_ref[rows, :]
        for sh in (1, 2, 4):
            keep = row >= sh
            u = jnp.where(keep, a * pltpu.roll(u, sh, 0) + u, u)
            a = jnp.where(keep, a * pltpu.roll(a, sh, 0), a)
        hh = a * h + u
        y_ref[rows, :] = hh * gate_ref[rows, :]
        return hh[SUBLANES - 1:SUBLANES, :]

    for s in range(n_seq):
        if tt == SUBLANES:
            h = tile(s * tt, h_ref[s])
        else:
            h = lax.fori_loop(0, tt // SUBLANES, lambda k, h, s=s: tile(s * tt + k * SUBLANES, h), h_ref[s],
                              unroll=LRU_SCAN_UNROLL)
        h_ref[s] = h
        hf_ref[s] = h


LRU_SCAN_UNROLL = 4


def _lru(proj, buf, h0, cw, cb, wa, ba, wx, bx, lam, layer, b, seq):
    n_seq, tt = (1, min(seq, 512)) if seq >= LANES else (LANES // seq, seq)
    nt = seq // tt
    rows = n_seq * tt
    small = lambda shape: pl.BlockSpec(shape, lambda i, j: (0,) * len(shape))
    return pl.pallas_call(
        functools.partial(_lru_body, n_seq=n_seq, tt=tt),
        grid=(b // n_seq, nt),
        in_specs=[
            pl.BlockSpec((rows, 512), lambda i, j: (i * nt + j, COL_XB)),
            pl.BlockSpec((rows, 512), lambda i, j: (i * nt + j, COL_GB)),
            pl.BlockSpec((None, n_seq, CONV_W - 1, W_B), lambda i, j: (layer, i, 0, 0)),
            pl.BlockSpec((None, n_seq, 1, W_B), lambda i, j: (layer, i, 0, 0)),
            small((CONV_W, W_B)), small((1, W_B)), small((W_B, W_B)), small((1, W_B)), small((W_B, W_B)),
            small((1, W_B)), small((1, W_B)),
        ],
        out_specs=[
            pl.BlockSpec((rows, 512), lambda i, j: (i * nt + j, 0)),
            pl.BlockSpec((n_seq, 1, W_B), lambda i, j: (i, 0, 0)),
        ],
        out_shape=[jax.ShapeDtypeStruct((b * seq, W_B), f32), jax.ShapeDtypeStruct((b, 1, W_B), f32)],
        scratch_shapes=[pltpu.VMEM((n_seq, tt + SUBLANES, W_B), f32), pltpu.VMEM((rows, W_B), f32),
                        pltpu.VMEM((rows, W_B), f32), pltpu.VMEM((n_seq, 1, W_B), f32)],
        compiler_params=_cparams(("parallel", "arbitrary")),
        name="lru",
    )(proj, proj, buf, h0, cw, cb, wa, ba, wx, bx, lam)


def _cmix_body(u_ref, v_ref, g_ref, w_ref, b_ref, y_ref, *maybe_vn_ref, n_chunks):
    vn = _rms(v_ref[...], g_ref[...])
    for vn_ref in maybe_vn_ref:
        vn_ref[...] = vn
    r = lax.broadcasted_iota(jnp.int32, (CHUNK_C, CHUNK_C), 0)
    c = lax.broadcasted_iota(jnp.int32, (CHUNK_C, CHUNK_C), 1)
    for g in range(G_C):
        sl = slice(g * LANES, (g + 1) * LANES)
        wg = jnp.where(c <= r, w_ref[g], 0.0).astype(bf16)
        bias = b_ref[:, g:g + 1]
        for ch in range(n_chunks):
            rows = slice(ch * CHUNK_C, (ch + 1) * CHUNK_C)
            mixed = _dot(wg, vn[rows, sl].astype(bf16)) + bias
            y_ref[rows, sl] = u_ref[rows, sl] * mixed


def _cmix(proj, g, w, bias, emit_vn):
    t = proj.shape[0]
    tc = min(t, 512)
    n_out = 2 if emit_vn else 1
    return pl.pallas_call(
        functools.partial(_cmix_body, n_chunks=tc // CHUNK_C),
        grid=(t // tc,),
        in_specs=[
            pl.BlockSpec((tc, 512), lambda i: (i, COL_UC)),
            pl.BlockSpec((tc, 512), lambda i: (i, COL_VC)),
            pl.BlockSpec((1, W_C), lambda i: (0, 0)),
            pl.BlockSpec((G_C, CHUNK_C, CHUNK_C), lambda i: (0, 0, 0)),
            pl.BlockSpec((CHUNK_C, G_C), lambda i: (0, 0)),
        ],
        out_specs=[pl.BlockSpec((tc, 512), lambda i: (i, 0))] * n_out,
        out_shape=[jax.ShapeDtypeStruct((t, W_C), f32)] * n_out,
        compiler_params=_cparams(("parallel",)),
        name="cmix",
    )(proj, proj, g, w, bias)


def _ssd_body(xbc_ref, z_ref, dt_ref, buf_ref, h0_ref, cw_ref, cb_ref, dtb_ref, alog_ref, d_ref, g_ref, hf_all_ref,
              y_ref, hf_ref, xx_ref, xc_ref, st_ref, *, n_seq, rows, n_chunk):
    del hf_all_ref
    ci = pl.program_id(1)
    q = SSD_CHUNK

    @pl.when(ci == 0)
    def _():
        st_ref[...] = h0_ref[...]

    per_seq = rows * n_chunk if n_seq == 1 else rows
    xbc = [_conv_tile(xbc_ref.at[pl.ds(s * per_seq, per_seq)], buf_ref.at[s], cw_ref, cb_ref, xx_ref.at[s], ci == 0,
                      per_seq) for s in range(n_seq)]
    xbc = xbc[0] if n_seq == 1 else jnp.concatenate(xbc, axis=0)
    xc_ref[...] = xbc * jax.nn.sigmoid(xbc)
    dtx = dt_ref[...] + dtb_ref[...]
    dt_all = jnp.maximum(dtx, 0.0) + jnp.log(1.0 + jnp.exp(-jnp.abs(dtx)))
    a_neg = -jnp.exp(alog_ref[...])
    ri = lax.broadcasted_iota(jnp.int32, (q, q), 0)
    cj = lax.broadcasted_iota(jnp.int32, (q, q), 1)
    causal = cj <= ri
    if n_seq > 1:
        causal = causal & ((ri // rows) == (cj // rows))
    exact_dot = functools.partial(jnp.dot, preferred_element_type=f32, precision=lax.Precision.HIGHEST)
    lane = lax.broadcasted_iota(jnp.int32, (q, LANES), 1)
    low = lane < P_D
    row_seq = lax.broadcasted_iota(jnp.int32, (q, LANES), 0) // rows
    rlow = lax.broadcasted_iota(jnp.int32, (LANES, N_D), 0) < P_D
    for c in range(n_chunk):
        cr = slice(c * q, (c + 1) * q)
        xbc = xc_ref[cr, :]
        dt = dt_all[cr]
        xs = xbc[:, :W_D]
        cum = exact_dot(jnp.where(causal, 1.0, 0.0).astype(f32), dt * a_neg)
        cum_t = cum.T
        if n_seq > 1:
            last_rows = exact_dot(jnp.where(cj == (ri // rows) * rows + (rows - 1), 1.0, 0.0).astype(f32), cum)
        else:
            last_rows = cum[q - 1:q, :]
        y_parts = []
        for g in range(G_D):
            bm = xbc[:, W_D + g * N_D:W_D + (g + 1) * N_D].astype(bf16)
            cm_f = xbc[:, W_D + (G_D + g) * N_D:W_D + (G_D + g + 1) * N_D]
            cm = cm_f.astype(bf16)
            scores = _dot_nt(cm, bm)
            for k in range(H_D // G_D // 2):
                h0 = g * (H_D // G_D) + 2 * k
                sl = slice(h0 * P_D, (h0 + 2) * P_D)
                col = lambda a, h: a[:, h:h + 1]
                pair = lambda a: jnp.where(low, col(a, h0), col(a, h0 + 1))
                xdt = xs[:, sl] * pair(dt)
                xdt_b = xdt.astype(bf16)
                yd = []
                for h in (h0, h0 + 1):
                    seg = col(cum, h) - cum_t[h:h + 1, :]
                    m = scores * jnp.exp(jnp.where(causal, seg, -jnp.inf))
                    yd.append(_dot(m.astype(bf16), xdt_b))
                xw = xdt * jnp.exp(pair(last_rows - cum))
                offs = []
                for s in range(n_seq):
                    rs = slice(s * rows, (s + 1) * rows)
                    st = st_ref[s, sl, :]
                    offs.append(_dot_nt(cm if n_seq == 1 else cm_f[rs].astype(bf16), st.astype(bf16)))
                    last = cum[(s + 1) * rows - 1:(s + 1) * rows, :]
                    dec = jnp.exp(jnp.where(rlow, last[:, h0:h0 + 1], last[:, h0 + 1:h0 + 2]))
                    xw_s = xw if n_seq == 1 else jnp.where(row_seq == s, xw, 0.0)
                    st_ref[s, sl, :] = dec * st + _dot_tn(xw_s.astype(bf16), bm)
                y_off = (offs[0] if n_seq == 1 else jnp.concatenate(offs, axis=0)) * jnp.exp(pair(cum))
                y_parts.append(jnp.where(low, yd[0], yd[1]) + y_off + d_ref[:, sl] * xs[:, sl])
        y = jnp.concatenate(y_parts, axis=1) * z_ref[cr, :]
        y_ref[cr, :] = _rms(y, g_ref[...])
    hf_ref[...] = st_ref[...]


SSD_CHUNKS_PER_STEP = 4


def _ssd(proj, dt_raw, buf, h0, cw, cb, dtb, alog, dvec, g, hf_all, layer, layer_out, b, seq):
    q = SSD_CHUNK
    if seq >= q:
        n_seq, rows = 1, q
        n_chunk = SSD_CHUNKS_PER_STEP if (seq // q) % SSD_CHUNKS_PER_STEP == 0 else 1
    else:
        n_seq, rows, n_chunk = q // seq, seq, 1
    blk = q * n_chunk
    nc = seq // blk if n_seq == 1 else 1
    small = lambda shape: pl.BlockSpec(shape, lambda i, j: (0,) * len(shape))
    return pl.pallas_call(
        functools.partial(_ssd_body, n_seq=n_seq, rows=rows, n_chunk=n_chunk),
        grid=(b // n_seq, nc),
        in_specs=[
            pl.BlockSpec((blk, XBC_D), lambda i, j: (i * nc + j, COL_XBC)),
            pl.BlockSpec((blk, 512), lambda i, j: (i * nc + j, COL_ZD)),
            pl.BlockSpec((blk, LANES), lambda i, j: (i * nc + j, 0)),
            pl.BlockSpec((None, n_seq, CONV_W - 1, XBC_D), lambda i, j: (layer, i, 0, 0)),
            pl.BlockSpec((None, n_seq, H_D * P_D, N_D), lambda i, j: (layer, i, 0, 0)),
            small((CONV_W, XBC_D)), small((1, XBC_D)), small((1, LANES)), small((1, LANES)), small((1, W_D)),
            small((1, W_D)), pl.BlockSpec(memory_space=pl.ANY),
        ],
        out_specs=[
            pl.BlockSpec((blk, W_D), lambda i, j: (i * nc + j, 0)),
            pl.BlockSpec((None, n_seq, H_D * P_D, N_D), lambda i, j: (layer_out, i, 0, 0)),
        ],
        out_shape=[jax.ShapeDtypeStruct((b * seq, W_D), f32), jax.ShapeDtypeStruct(hf_all.shape, f32)],
        input_output_aliases={11: 1},
        scratch_shapes=[pltpu.VMEM((n_seq, blk // n_seq + SUBLANES, XBC_D), f32), pltpu.VMEM((blk, XBC_D), f32),
                        pltpu.VMEM((n_seq, H_D * P_D, N_D), f32)],
        compiler_params=_cparams(("parallel", "arbitrary")),
        name="ssd",
    )(proj, proj, dt_raw, buf, h0, cw, cb, dtb, alog, dvec, g, hf_all)


def _merge_body(x_ref, g_ref, ya_ref, yb_ref, yc_ref, yd_ref, wg_ref, wb_ref, wo_ref, o_ref):
    x = x_ref[...]
    hn = _rms(x, g_ref[...]).astype(bf16)
    m = None
    for k, y_ref in enumerate((ya_ref, yb_ref, yc_ref, yd_ref)):
        gate = jax.nn.sigmoid(_dot(hn, wg_ref[:, k * D_MODEL:(k + 1) * D_MODEL]))
        t = gate * _dot(y_ref[...].astype(bf16), wb_ref[k])
        m = t if m is None else m + t
    o_ref[...] = x + _dot(m.astype(bf16), wo_ref[...])


def _merge(x, g, ya, yb, yc, yd, wg, wb, wo):
    t = x.shape[0]
    tm = min(t, ROW_TILE)
    row = lambda w: pl.BlockSpec((tm, w), lambda i: (i, 0))
    return pl.pallas_call(
        _merge_body,
        grid=(t // tm,),
        in_specs=[row(D_MODEL), _resident((1, D_MODEL), lambda i: (0, 0)), row(512), row(512), row(512), row(512),
                  _resident((D_MODEL, N_BRANCH * D_MODEL), lambda i: (0, 0)),
                  _resident((N_BRANCH, 512, D_MODEL), lambda i: (0, 0, 0)),
                  _resident((D_MODEL, D_MODEL), lambda i: (0, 0))],
        out_specs=row(D_MODEL),
        out_shape=jax.ShapeDtypeStruct((t, D_MODEL), f32),
        compiler_params=_cparams(("parallel",)),
        name="merge",
    )(x, g, ya, yb, yc, yd, wg, wb, wo)


def _memkv_body(m_ref, wk_ref, wv_ref, g_ref, k_ref, v_ref, kt_ref, vt_ref):
    mb = m_ref[...].astype(bf16)
    rows = mb.shape[0]
    kk = _dot(mb, wk_ref[...])
    vv = _dot(mb, wv_ref[...])
    v_ref[...] = vv
    for h in range(XH):
        sl = slice(h * XDH, (h + 1) * XDH)
        kh = _rms(kk[:, sl], g_ref[...])
        k_ref[:, sl] = kh
        kt_ref[pl.ds(h, rows, stride=XH), :] = kh
        vt_ref[pl.ds(h, rows, stride=XH), :] = vv[:, sl]


def _memkv(mem, wk, wv, g):
    r = mem.shape[0]
    tm = min(r, 512)
    return pl.pallas_call(
        _memkv_body,
        grid=(DEPTH, r // tm),
        in_specs=[pl.BlockSpec((tm, D_MODEL), lambda l, i: (i, 0)),
                  pl.BlockSpec((None, D_MODEL, XW), lambda l, i: (l, 0, 0)),
                  pl.BlockSpec((None, D_MODEL, XW), lambda l, i: (l, 0, 0)),
                  pl.BlockSpec((None, 1, XDH), lambda l, i: (l, 0, 0))],
        out_specs=[pl.BlockSpec((None, tm, XW), lambda l, i: (l, i, 0))] * 2
        + [pl.BlockSpec((None, tm * XH, XDH), lambda l, i: (l, i, 0))] * 2,
        out_shape=[jax.ShapeDtypeStruct((DEPTH, r, XW), f32)] * 2 + [jax.ShapeDtypeStruct((DEPTH, r * XH, XDH), f32)] * 2,
        compiler_params=_cparams(("parallel", "parallel")),
        name="memkv",
    )(mem, wk, wv, g)


def _cross_body(x_ref, g_ref, wq_ref, qg_ref, k_ref, v_ref, wo_ref, o_ref, *, n_seq, rows, tall):
    x = x_ref[...]
    qx = _dot(_rms(x, g_ref[...]).astype(bf16), wq_ref[...])
    qn = [_rms(qx[:, h * XDH:(h + 1) * XDH], qg_ref[...]) for h in range(XH)]

    def attend(sc, v):
        p = jnp.exp(sc - jnp.max(sc, axis=-1, keepdims=True))
        return _dot(p.astype(bf16), v) * (1.0 / jnp.sum(p, axis=-1, keepdims=True))

    if tall:
        r = lax.broadcasted_iota(jnp.int32, (XH * rows, N_MEM * XH), 0)
        c = lax.broadcasted_iota(jnp.int32, (XH * rows, N_MEM * XH), 1)
        same_head = (c % XH) == (r // rows)
        outs = []
        for s in range(n_seq):
            qs = jnp.concatenate([q[s * rows:(s + 1) * rows] for q in qn], axis=0)
            sc = _dot_nt(qs.astype(bf16), k_ref[s].astype(bf16)) * (XDH ** -0.5)
            o = attend(jnp.where(same_head, sc, -jnp.inf), v_ref[s].astype(bf16))
            outs.append(jnp.concatenate([o[h * rows:(h + 1) * rows] for h in range(XH)], axis=1))
        ox = jnp.concatenate(outs, axis=0)
    else:
        heads = []
        for h in range(XH):
            sl = slice(h * XDH, (h + 1) * XDH)
            outs = []
            for s in range(n_seq):
                sc = _dot_nt(qn[h][s * rows:(s + 1) * rows].astype(bf16), k_ref[s, :, sl].astype(bf16)) * (XDH ** -0.5)
                outs.append(attend(sc, v_ref[s, :, sl].astype(bf16)))
            heads.append(outs[0] if n_seq == 1 else jnp.concatenate(outs, axis=0))
        ox = jnp.concatenate(heads, axis=1)
    o_ref[...] = x + _dot(ox.astype(bf16), wo_ref[...])


def _cross(x, g, wq, qg, mem_k, mem_v, wo, layer, seq, tall):
    t = x.shape[0]
    if seq >= 512:
        n_seq, rows = 1, 512
        per = seq // rows
        mem_idx = lambda i: (layer, i // per, 0, 0)
    else:
        n_seq, rows = 8, seq
        mem_idx = lambda i: (layer, i, 0, 0)
    tm = n_seq * rows
    small = lambda shape: pl.BlockSpec(shape, lambda i: (0,) * len(shape))
    mem_block = (None, n_seq) + tuple(mem_k.shape[2:])
    return pl.pallas_call(
        functools.partial(_cross_body, n_seq=n_seq, rows=rows, tall=tall),
        grid=(t // tm,),
        in_specs=[pl.BlockSpec((tm, D_MODEL), lambda i: (i, 0)), small((1, D_MODEL)), small((D_MODEL, XW)),
                  small((1, XDH)), pl.BlockSpec(mem_block, mem_idx), pl.BlockSpec(mem_block, mem_idx),
                  small((XW, D_MODEL))],
        out_specs=pl.BlockSpec((tm, D_MODEL), lambda i: (i, 0)),
        out_shape=jax.ShapeDtypeStruct((t, D_MODEL), f32),
        compiler_params=_cparams(("parallel",)),
        name="cross",
    )(x, g, wq, qg, mem_k, mem_v, wo)


def _ffn_body(x_ref, g_ref, wg_ref, wu_ref, wo_ref, o_ref):
    x = x_ref[...]
    hn = _rms(x, g_ref[...]).astype(bf16)
    gate = _dot(hn, wg_ref[...])
    up = _dot(hn, wu_ref[...])
    act = (gate * jax.nn.sigmoid(gate) * up).astype(bf16)
    o_ref[...] = x + _dot(act, wo_ref[...])


def _ffn(x, g, w_in, w_out):
    t = x.shape[0]
    tm = min(t, ROW_TILE)
    return pl.pallas_call(
        _ffn_body,
        grid=(t // tm,),
        in_specs=[pl.BlockSpec((tm, D_MODEL), lambda i: (i, 0)),
                  _resident((1, D_MODEL), lambda i: (0, 0)),
                  _resident((D_MODEL, D_FF), lambda i: (0, 0)),
                  _resident((D_MODEL, D_FF), lambda i: (0, 1)),
                  _resident((D_FF, D_MODEL), lambda i: (0, 0))],
        out_specs=pl.BlockSpec((tm, D_MODEL), lambda i: (i, 0)),
        out_shape=jax.ShapeDtypeStruct((t, D_MODEL), f32),
        compiler_params=_cparams(("parallel",)),
        name="ffn",
    )(x, g, w_in, w_in, w_out)


def _rope_tables(pos):
    half = DH_A // 2
    inv = ROPE_THETA ** (-jnp.arange(half, dtype=f32) / half)
    ang = pos.astype(f32)[:, None] * inv[None, :]
    cos, sin = jnp.cos(ang), jnp.sin(ang)
    cos_h = jnp.concatenate([cos, cos], axis=-1)
    sin_h = jnp.concatenate([-sin, sin], axis=-1)
    reps = W_A // DH_A
    return jnp.tile(cos_h, (1, reps)), jnp.tile(sin_h, (1, reps))


def _block_diag(w):
    nb, a, _ = w.shape
    eye = jnp.eye(nb, dtype=w.dtype)
    return (eye[:, None, :, None] * w[:, :, None, :]).reshape(nb * a, nb * a)


def _layer_params(l, P):
    w_in = P['w_in'][l]
    split_dt = 8 * 512 + XBC_D
    lp = {
        'norm_mix': P['norm_mix'][l][None],
        'w_main': w_in[:, :split_dt].astype(bf16),
        'w_gates': w_in[:, split_dt + H_D:].astype(bf16),
        'w_dt': jnp.pad(w_in[:, split_dt:split_dt + H_D], ((0, 0), (0, LANES - H_D))).astype(bf16),
        'q_gain': jnp.tile(P['q_norm'][l], 2)[None],
        'k_gain': jnp.tile(P['k_norm'][l], 2)[None],
        'lambda_qk': P['lambda_qk'][l],
        'subln': P['subln'][l].reshape(1, W_A),
        'lru_cw': P['lru_conv_w'][l], 'lru_cb': P['lru_conv_b'][l][None],
        'lru_wa': _block_diag(P['lru_wa'][l]).astype(bf16), 'lru_ba': P['lru_ba'][l][None],
        'lru_wx': _block_diag(P['lru_wx'][l]).astype(bf16), 'lru_bx': P['lru_bx'][l][None],
        'lru_lambda': P['lru_lambda'][l][None],
        'sgu_norm': P['sgu_norm'][l][None],
        'sgu_w': P['sgu_w'][l], 'sgu_b': P['sgu_b'][l],
        'ssd_cw': P['ssd_conv_w'][l], 'ssd_cb': P['ssd_conv_b'][l][None],
        'ssd_dtb': jnp.pad(P['ssd_dt_bias'][l], (0, LANES - H_D))[None],
        'ssd_alog': jnp.pad(P['ssd_a_log'][l], (0, LANES - H_D))[None],
        'ssd_d': jnp.repeat(P['ssd_d'][l], P_D)[None],
        'ssd_norm': P['ssd_norm'][l][None],
        'w_branch': P['w_branch'][l].astype(bf16), 'w_o': P['w_o'][l].astype(bf16),
        'norm_cross': P['norm_cross'][l][None], 'w_xq': P['w_xq'][l].astype(bf16),
        'xq_norm': P['xq_norm'][l][None], 'w_xo': P['w_xo'][l].astype(bf16),
        'norm_ffn': P['norm_ffn'][l][None], 'w_ffn_in': P['w_ffn_in'][l].astype(bf16),
        'w_ffn_out': P['w_ffn_out'][l].astype(bf16),
    }
    return lp


def _sgu_weights(lp, seq):
    w, bias = lp['sgu_w'], lp['sgu_b']
    if seq >= CHUNK_C:
        return w, bias.T
    reps = CHUNK_C // seq
    eye = jnp.eye(reps, dtype=w.dtype)
    wk = (eye[None, :, None, :, None] * w[:, None, :seq, None, :seq]).reshape(G_C, CHUNK_C, CHUNK_C)
    return wk, jnp.tile(bias[:, :seq], (1, reps)).T


def _head_bd():
    i = jnp.arange(LANES) // DH_A
    return jnp.where(i[:, None] == i[None, :], 1.0 / DH_A, 0.0).astype(bf16)


def _layer(l, lp, x, b, seq, rope_tab, states, sl, mem_k, mem_v, ml, mem_tall, stacked, paged=None):
    lru_h0, lru_buf, ssm_h0, ssm_buf = states
    v_all, ssm_all = stacked
    lam_init = 0.8 - 0.6 * math.exp(-0.3 * l)
    proj, dt_raw = _in_proj(x, lp['norm_mix'], lp['w_main'], lp['w_dt'])
    cos, sin = rope_tab
    act = f32 if paged is not None else bf16
    q_scale = DH_A ** -0.5 * (1.0 if paged is not None else math.log2(math.e))
    q, k, kb, vb, v_all = _qk_prep(proj, cos, sin, lp['q_gain'], lp['k_gain'], _head_bd(), act, q_scale, v_all, l)
    if paged is None:
        y_a = _attn_prompt(lp['lambda_qk'], q, kb, vb, lp['subln'], lam_init, b, seq)
    else:
        page_table, cache_kt, cache_v2 = paged
        y_a = _attn_sample(page_table, lp['lambda_qk'], q, kb, vb, lp['subln'], cache_kt, cache_v2, l, lam_init, b, seq)
    y_b, lru_h = _lru(proj, lru_buf, lru_h0, lp['lru_cw'], lp['lru_cb'], lp['lru_wa'], lp['lru_ba'], lp['lru_wx'],
                      lp['lru_bx'], lp['lru_lambda'], sl, b, seq)
    sgu_w, sgu_b = _sgu_weights(lp, seq)
    y_c, *vc = _cmix(proj, lp['sgu_norm'], sgu_w, sgu_b, emit_vn=paged is not None)
    vc = vc[0] if vc else None
    y_d, ssm_all = _ssd(proj, dt_raw, ssm_buf, ssm_h0, lp['ssd_cw'], lp['ssd_cb'], lp['ssd_dtb'], lp['ssd_alog'],
                        lp['ssd_d'], lp['ssd_norm'], ssm_all, sl, l, b, seq)
    x = _merge(x, lp['norm_mix'], y_a, y_b, y_c, y_d, lp['w_gates'], lp['w_branch'], lp['w_o'])
    x = _cross(x, lp['norm_cross'], lp['w_xq'], lp['xq_norm'], mem_k, mem_v, lp['w_xo'], ml, seq, mem_tall)
    x = _ffn(x, lp['norm_ffn'], lp['w_ffn_in'], lp['w_ffn_out'])
    p3 = proj.reshape(b, seq, N_MAIN)
    assert seq >= CONV_W - 1
    new_lru_buf = p3[:, seq - (CONV_W - 1):, COL_XB * 512:(COL_XB + 1) * 512]
    new_ssm_buf = p3[:, seq - (CONV_W - 1):, COL_XBC * 1024:(COL_XBC + 1) * 1024]
    return x, (k, lru_h, new_lru_buf, new_ssm_buf, vc), (v_all, ssm_all)


def kernel(x_prompt, x_sample, mem_prompt, cache_attn_k, cache_attn_v, page_table, cache_mem_k, cache_mem_v, state_lru, state_lru_conv, state_ssm, state_ssm_conv, norm_mix, w_in, q_norm, k_norm, lambda_qk, subln, lru_conv_w, lru_conv_b, lru_wa, lru_ba, lru_wx, lru_bx, lru_lambda, sgu_norm, sgu_w, sgu_b, ssd_conv_w, ssd_conv_b, ssd_dt_bias, ssd_a_log, ssd_d, ssd_norm, w_branch, w_o, norm_cross, w_xq, w_xk, w_xv, xq_norm, xk_norm, w_xo, norm_ffn, w_ffn_in, w_ffn_out):
    P = dict(norm_mix=norm_mix, w_in=w_in, q_norm=q_norm, k_norm=k_norm, lambda_qk=lambda_qk, subln=subln,
             lru_conv_w=lru_conv_w, lru_conv_b=lru_conv_b, lru_wa=lru_wa, lru_ba=lru_ba, lru_wx=lru_wx, lru_bx=lru_bx,
             lru_lambda=lru_lambda, sgu_norm=sgu_norm, sgu_w=sgu_w, sgu_b=sgu_b, ssd_conv_w=ssd_conv_w,
             ssd_conv_b=ssd_conv_b, ssd_dt_bias=ssd_dt_bias, ssd_a_log=ssd_a_log, ssd_d=ssd_d, ssd_norm=ssd_norm,
             w_branch=w_branch, w_o=w_o, norm_cross=norm_cross, w_xq=w_xq, xq_norm=xq_norm, w_xo=w_xo,
             norm_ffn=norm_ffn, w_ffn_in=w_ffn_in, w_ffn_out=w_ffn_out)
    bp, lp_, _ = x_prompt.shape
    bs, ls, _ = x_sample.shape
    n_pages = page_table.shape[1]
    past_len = n_pages * PAGE_SIZE
    n_phys = cache_attn_k.shape[1]

    rope_p = _rope_tables(jnp.arange(lp_))
    cos_s, sin_s = _rope_tables(past_len + jnp.arange(ls))
    rope_s = (jnp.tile(cos_s, (bs, 1)), jnp.tile(sin_s, (bs, 1)))
    cache_kt = jnp.transpose(cache_attn_k, (0, 1, 3, 4, 2))
    cache_v2 = cache_attn_v.reshape(DEPTH, n_phys, PAGE_SIZE * H_A, DV_A)
    mem_k_s = cache_mem_k.reshape(DEPTH, bs, N_MEM * XH, XDH)
    mem_v_s = cache_mem_v.reshape(DEPTH, bs, N_MEM * XH, XDH)

    pmk, pmv, pmk_t, pmv_t = _memkv(mem_prompt.reshape(bp * N_MEM, D_MODEL), w_xk.astype(bf16), w_xv.astype(bf16),
                                    xk_norm[:, None, :])

    yp = x_prompt.reshape(bp * lp_, D_MODEL)
    ys = x_sample.reshape(bs * ls, D_MODEL)
    zeros = lambda *s: jnp.zeros(s, f32)
    states_p = (zeros(1, bp, 1, W_B), zeros(1, bp, CONV_W - 1, W_B), zeros(1, bp, H_D * P_D, N_D),
                zeros(1, bp, CONV_W - 1, XBC_D))
    states_s = (state_lru[:, :, None, :], state_lru_conv, state_ssm.reshape(DEPTH, bs, H_D * P_D, N_D), state_ssm_conv)
    pmk4 = pmk.reshape(DEPTH, bp, N_MEM, XW)
    pmv4 = pmv.reshape(DEPTH, bp, N_MEM, XW)
    stacked_p = (zeros(DEPTH, bp * lp_ * H_A, DV_A), zeros(DEPTH, bp, H_D * P_D, N_D))
    stacked_s = (zeros(DEPTH, bs * ls * H_A, DV_A), zeros(DEPTH, bs, H_D * P_D, N_D))
    p_states, s_states = [], []
    for l in range(DEPTH):
        lp = _layer_params(l, P)
        yp, sp, stacked_p = _layer(l, lp, yp, bp, lp_, rope_p, states_p, 0, pmk4, pmv4, l, False, stacked_p)
        p_states.append(sp)
        ys, ss, stacked_s = _layer(l, lp, ys, bs, ls, rope_s, states_s, l, mem_k_s, mem_v_s, l, True, stacked_s,
                                   paged=(page_table, cache_kt, cache_v2))
        s_states.append(ss)

    def stack(states, idx, shape):
        return jnp.stack([s[idx] for s in states]).reshape((DEPTH,) + shape)

    return (
        yp.reshape(bp, lp_, D_MODEL), ys.reshape(bs, ls, D_MODEL),
        stack(p_states, 0, (bp, lp_, 2 * H_A, DH_A)), stacked_p[0].reshape(DEPTH, bp, lp_, H_A, DV_A),
        pmk_t.reshape(DEPTH, bp, N_MEM, XH, XDH), pmv_t.reshape(DEPTH, bp, N_MEM, XH, XDH),
        stack(p_states, 1, (bp, W_B)), stack(p_states, 2, (bp, CONV_W - 1, W_B)),
        stacked_p[1].reshape(DEPTH, bp, H_D, P_D, N_D), stack(p_states, 3, (bp, CONV_W - 1, XBC_D)),
        stack(s_states, 0, (bs, ls, 2 * H_A, DH_A)), stacked_s[0].reshape(DEPTH, bs, ls, H_A, DV_A),
        stack(s_states, 1, (bs, W_B)), stack(s_states, 2, (bs, CONV_W - 1, W_B)),
        stacked_s[1].reshape(DEPTH, bs, H_D, P_D, N_D), stack(s_states, 3, (bs, CONV_W - 1, XBC_D)),
        stack(s_states, 4, (bs, ls, W_C)),
    )
```

```python
import functools
import math

import jax
import jax.numpy as jnp
from jax import lax
from jax.experimental import pallas as pl
from jax.experimental.pallas import tpu as pltpu

f32 = jnp.float32
bf16 = jnp.bfloat16

D_MODEL = 1024
DEPTH = 4
PAGE_SIZE = 128
H_A = 4
DH_A = 64
DV_A = 128
W_A = 512
ROPE_THETA = 10000.0
W_B = 512
NB_B = 8
LRU_C = 8.0
CONV_W = 4
W_C = 512
G_C = 4
CHUNK_C = 128
H_D = 8
P_D = 64
W_D = 512
G_D = 2
N_D = 128
SSD_CHUNK = 128
XBC_D = 1024
N_MEM = 256
XH = 4
XDH = 128
XW = 512
D_FF = 2816
EPS = 1e-6

LANES = 128
SUBLANES = 8
VMEM_LIMIT = 48 * 1024 * 1024

N_BRANCH = 4
N_MAIN = 5120
COL_Q, COL_K, COL_V, COL_XB, COL_GB, COL_UC, COL_VC, COL_ZD = range(8)
COL_XBC = 4


ROW_TILE = 512


def _cparams(sem):
    return pltpu.CompilerParams(dimension_semantics=sem, vmem_limit_bytes=VMEM_LIMIT)


def _resident(shape, index_map):
    return pl.BlockSpec(shape, index_map, pipeline_mode=pl.Buffered(1))


def _dot(a, b):
    return jnp.dot(a, b, preferred_element_type=f32)


def _dot_nt(a, b):
    return lax.dot_general(a, b, (((1,), (1,)), ((), ())), preferred_element_type=f32)


def _dot_tn(a, b):
    return lax.dot_general(a, b, (((0,), (0,)), ((), ())), preferred_element_type=f32)


def _rms(x, g):
    ms = jnp.mean(x * x, axis=-1, keepdims=True)
    return x * lax.rsqrt(ms + EPS) * g


def _softplus(x):
    return jnp.maximum(x, 0.0) + jnp.log1p(jnp.exp(-jnp.abs(x)))


def _in_proj_body(x_ref, g_ref, w_ref, wdt_ref, proj_ref, dt_ref):
    hn = _rms(x_ref[...], g_ref[...]).astype(bf16)
    dt_ref[...] = _dot(hn, wdt_ref[...])
    for c in range(N_MAIN // 512):
        sl = slice(c * 512, (c + 1) * 512)
        acc = _dot(hn, w_ref[:, sl])
        if COL_GB <= c <= COL_VC:
            acc = jax.nn.gelu(acc)
        elif c == COL_ZD:
            acc = acc * jax.nn.sigmoid(acc)
        proj_ref[:, sl] = acc


def _in_proj(x, g, w_main, w_dt, layer):
    t = x.shape[0]
    tm = min(t, ROW_TILE)
    return pl.pallas_call(
        _in_proj_body,
        grid=(t // tm,),
        in_specs=[
            pl.BlockSpec((tm, D_MODEL), lambda i: (i, 0)),
            _resident((1, D_MODEL), lambda i: (0, 0)),
            _resident((None, D_MODEL, N_MAIN), lambda i: (layer, 0, 0)),
            _resident((None, D_MODEL, LANES), lambda i: (layer, 0, 0)),
        ],
        out_specs=[
            pl.BlockSpec((tm, N_MAIN), lambda i: (i, 0)),
            pl.BlockSpec((tm, LANES), lambda i: (i, 0)),
        ],
        out_shape=[jax.ShapeDtypeStruct((t, N_MAIN), f32), jax.ShapeDtypeStruct((t, LANES), f32)],
        compiler_params=_cparams(("parallel",)),
        name="in_proj",
    )(x, g, w_main, w_dt)


def _head_norm_rope(x, gain, cos, sin, bd):
    sq = x * x
    hi = sq.astype(bf16)
    lo = (sq - hi.astype(f32)).astype(bf16)
    ms = _dot(hi, bd) + _dot(lo, bd)
    y = x * lax.rsqrt(ms + EPS) * gain
    lane = lax.broadcasted_iota(jnp.int32, y.shape, 1)
    first_half = (lane % DH_A) < (DH_A // 2)
    swapped = jnp.where(first_half, pltpu.roll(y, LANES - DH_A // 2, 1), pltpu.roll(y, DH_A // 2, 1))
    return y * cos + swapped * sin


def _qk_prep_body(q_ref, k_ref, v_ref, cos_ref, sin_ref, qg_ref, kg_ref, bd_ref, v_all_ref, qo_ref, ko_ref, kb_ref,
                  vb_ref, vt_ref, *, q_scale):
    del v_all_ref
    bd = bd_ref[...]
    rows = v_ref.shape[0]
    for h in range(H_A):
        vt_ref[pl.ds(h, rows, stride=H_A), :] = v_ref[:, h * DV_A:(h + 1) * DV_A]
    for c in range(W_A // LANES):
        sl = slice(c * LANES, (c + 1) * LANES)
        cos = cos_ref[:, sl]
        sin = sin_ref[:, sl]
        q = _head_norm_rope(q_ref[:, sl], qg_ref[...], cos, sin, bd)
        k = _head_norm_rope(k_ref[:, sl], kg_ref[...], cos, sin, bd)
        qo_ref[:, sl] = (q * q_scale).astype(qo_ref.dtype)
        ko_ref[:, sl] = k
        kb_ref[:, sl] = k.astype(kb_ref.dtype)
    vb_ref[...] = v_ref[...].astype(vb_ref.dtype)


def _qk_prep(proj, cos, sin, qg, kg, bd, act_dtype, q_scale, v_all, layer):
    t = proj.shape[0]
    p = cos.shape[0]
    tm = min(t, p, 512)
    npos = p // tm
    row = lambda c: pl.BlockSpec((tm, 512), lambda i, c=c: (i, c))
    tab = pl.BlockSpec((tm, 512), lambda i: (i % npos, 0))
    small = lambda shape: pl.BlockSpec(shape, lambda i: (0, 0))
    return pl.pallas_call(
        functools.partial(_qk_prep_body, q_scale=q_scale),
        grid=(t // tm,),
        in_specs=[row(COL_Q), row(COL_K), row(COL_V), tab, tab, small((1, LANES)), small((1, LANES)),
                  small((LANES, LANES)), pl.BlockSpec(memory_space=pl.ANY)],
        out_specs=[row(0)] * 4 + [pl.BlockSpec((None, tm * H_A, DV_A), lambda i: (layer, i, 0))],
        out_shape=[jax.ShapeDtypeStruct((t, 512), act_dtype), jax.ShapeDtypeStruct((t, 512), f32),
                   jax.ShapeDtypeStruct((t, 512), act_dtype), jax.ShapeDtypeStruct((t, 512), act_dtype),
                   jax.ShapeDtypeStruct(v_all.shape, f32)],
        input_output_aliases={8: 4},
        compiler_params=_cparams(("parallel",)),
        name="qk_prep",
    )(proj, proj, proj, cos, sin, qg, kg, bd, v_all)


def _lambda_value(lq, lam_init):
    a = jnp.sum(lq[0:1] * lq[1:2], axis=-1, keepdims=True)
    b = jnp.sum(lq[2:3] * lq[3:4], axis=-1, keepdims=True)
    return jnp.exp(a) - jnp.exp(b) + lam_init


def _stack_pair(qp):
    lane = lax.broadcasted_iota(jnp.int32, qp.shape, 1)
    zero = jnp.zeros_like(qp)
    return jnp.concatenate([jnp.where(lane < DH_A, qp, zero), jnp.where(lane >= DH_A, qp, zero)], axis=0)


def _diff_finish(acc, l, lam, sub, lam_init, tq):
    o = acc[:tq] / l[:tq] - lam * (acc[tq:] / l[tq:])
    return _rms(o, sub) * (1.0 - lam_init)


def _attn_prompt_body(lq_ref, q_ref, k_ref, v_ref, sub_ref, o_ref, *, lam_init, tq, tk):
    qi = pl.program_id(1)
    lam = _lambda_value(lq_ref[...], lam_init)
    per = tk // tq
    n_full = qi // per
    r = lax.broadcasted_iota(jnp.int32, (2 * tq, tk), 0)
    r = jnp.where(r >= tq, r - tq, r) + (qi % per) * tq
    c = lax.broadcasted_iota(jnp.int32, (2 * tq, tk), 1)
    diag = c <= r
    sls = [slice(h * LANES, (h + 1) * LANES) for h in range(H_A)]
    qqs = [_stack_pair(q_ref[:, sl]) for sl in sls]

    def step(kv, carry, masked):
        rows = pl.ds(pl.multiple_of(kv * tk, tk), tk)
        out = []
        for (m, l, acc), qq, sl in zip(carry, qqs, sls):
            s = _dot_nt(qq, k_ref[rows, sl])
            if masked:
                s = jnp.where(diag, s, -jnp.inf)
            m_new = jnp.maximum(m, jnp.max(s, axis=-1, keepdims=True))
            alpha = jnp.exp2(m - m_new)
            p = jnp.exp2(s - m_new)
            l = alpha * l + jnp.sum(p, axis=-1, keepdims=True)
            acc = alpha * acc + _dot(p.astype(bf16), v_ref[rows, sl])
            out.append((m_new, l, acc))
        return tuple(out)

    init = tuple((jnp.full((2 * tq, 1), -jnp.inf, f32), jnp.zeros((2 * tq, 1), f32), jnp.zeros((2 * tq, LANES), f32))
                 for _ in range(H_A))
    carry = lax.fori_loop(0, n_full, functools.partial(step, masked=False), init)
    carry = step(n_full, carry, True)
    for (_, l, acc), sl in zip(carry, sls):
        o_ref[:, sl] = _diff_finish(acc, l, lam, sub_ref[:, sl], lam_init, tq)


ATTN_TQ = 512
ATTN_TK = 512


def _attn_prompt(lq, q, k, v, sub, lam_init, b, seq):
    tq = min(seq, ATTN_TQ)
    tk = min(seq, ATTN_TK)
    nq = seq // tq
    return pl.pallas_call(
        functools.partial(_attn_prompt_body, lam_init=lam_init, tq=tq, tk=tk),
        grid=(b, nq),
        in_specs=[
            pl.BlockSpec((4, DH_A), lambda i, j: (0, 0)),
            pl.BlockSpec((tq, 512), lambda i, j: (i * nq + j, 0)),
            pl.BlockSpec((seq, 512), lambda i, j: (i, 0)),
            pl.BlockSpec((seq, 512), lambda i, j: (i, 0)),
            pl.BlockSpec((1, 512), lambda i, j: (0, 0)),
        ],
        out_specs=pl.BlockSpec((tq, 512), lambda i, j: (i * nq + j, 0)),
        out_shape=jax.ShapeDtypeStruct((b * seq, 512), f32),
        compiler_params=_cparams(("parallel", "arbitrary")),
        name="attn_prompt",
    )(lq, q, k, v, sub)


def _attn_sample_body(pt_ref, lq_ref, q_ref, kn_ref, vn_ref, sub_ref, *rest, lam_init, n_pages, ls):
    del pt_ref
    k_refs = rest[:n_pages]
    v_refs = rest[n_pages:2 * n_pages]
    o_ref = rest[2 * n_pages]
    lam = _lambda_value(lq_ref[...], lam_init)
    pad = jnp.zeros((PAGE_SIZE - ls, LANES), f32)
    r = lax.broadcasted_iota(jnp.int32, (2 * ls, PAGE_SIZE), 0)
    r = jnp.where(r >= ls, r - ls, r)
    c = lax.broadcasted_iota(jnp.int32, (2 * ls, PAGE_SIZE), 1)
    new_mask = c <= r
    for h in range(H_A):
        sl = slice(h * LANES, (h + 1) * LANES)
        qq = _stack_pair(q_ref[:, sl]).astype(bf16)
        kt = jnp.concatenate([k_refs[p][2 * h:2 * h + 2].reshape(LANES, PAGE_SIZE) for p in range(n_pages)], axis=1)
        s_past = _dot(qq, kt.astype(bf16))
        k_new = jnp.concatenate([kn_ref[:, sl], pad], axis=0).astype(bf16)
        s_new = jnp.where(new_mask, _dot_nt(qq, k_new), -jnp.inf)
        m = jnp.maximum(jnp.max(s_past, axis=-1, keepdims=True), jnp.max(s_new, axis=-1, keepdims=True))
        p_past = jnp.exp(s_past - m)
        p_new = jnp.exp(s_new - m)
        l = jnp.sum(p_past, axis=-1, keepdims=True) + jnp.sum(p_new, axis=-1, keepdims=True)
        v_past = jnp.concatenate([v_refs[p][pl.ds(h, PAGE_SIZE, stride=H_A), :] for p in range(n_pages)], axis=0)
        v_new = jnp.concatenate([vn_ref[:, sl], pad], axis=0).astype(bf16)
        acc = _dot(p_past.astype(bf16), v_past.astype(bf16)) + _dot(p_new.astype(bf16), v_new)
        o_ref[:, sl] = _diff_finish(acc, l, lam, sub_ref[:, sl], lam_init, ls)


def _attn_sample(page_table, lq, q, k_new, v_new, sub, cache_kt, cache_v2, layer, lam_init, b, ls):
    n_pages = page_table.shape[1]
    row = pl.BlockSpec((ls, 512), lambda i, pt: (i, 0))
    k_specs = [pl.BlockSpec((None, None, 2 * H_A, DH_A, PAGE_SIZE), lambda i, pt, p=p: (layer, pt[i, p], 0, 0, 0))
               for p in range(n_pages)]
    v_specs = [pl.BlockSpec((None, None, PAGE_SIZE * H_A, DV_A), lambda i, pt, p=p: (layer, pt[i, p], 0, 0))
               for p in range(n_pages)]
    grid_spec = pltpu.PrefetchScalarGridSpec(
        num_scalar_prefetch=1,
        grid=(b,),
        in_specs=[pl.BlockSpec((4, DH_A), lambda i, pt: (0, 0)), row, row, row,
                  pl.BlockSpec((1, 512), lambda i, pt: (0, 0))] + k_specs + v_specs,
        out_specs=row,
    )
    return pl.pallas_call(
        functools.partial(_attn_sample_body, lam_init=lam_init, n_pages=n_pages, ls=ls),
        grid_spec=grid_spec,
        out_shape=jax.ShapeDtypeStruct((b * ls, 512), f32),
        compiler_params=_cparams(("parallel",)),
        name="attn_sample",
    )(page_table, lq, q, k_new, v_new, sub, *([cache_kt] * n_pages), *([cache_v2] * n_pages))


def _conv_tile(x_ref, buf_ref, w_ref, b_ref, xx_ref, first, tt):
    @pl.when(first)
    def _():
        xx_ref[SUBLANES - (CONV_W - 1):SUBLANES, :] = buf_ref[...]

    @pl.when(jnp.logical_not(first))
    def _():
        xx_ref[SUBLANES - (CONV_W - 1):SUBLANES, :] = xx_ref[tt + SUBLANES - (CONV_W - 1):tt + SUBLANES, :]

    xx_ref[SUBLANES:SUBLANES + tt, :] = x_ref[...]
    y = b_ref[...] + w_ref[0:1, :] * xx_ref[SUBLANES - 3:SUBLANES - 3 + tt, :]
    for j in range(1, CONV_W):
        y = y + w_ref[j:j + 1, :] * xx_ref[SUBLANES - 3 + j:SUBLANES - 3 + j + tt, :]
    return y


def _lru_body(x_ref, gate_ref, buf_ref, h0_ref, cw_ref, cb_ref, wa_ref, ba_ref, wx_ref, bx_ref, lam_ref,
              y_ref, hf_ref, xx_ref, a_ref, u_ref, h_ref, *, n_seq, tt):
    ti = pl.program_id(1)
    xc = [_conv_tile(x_ref.at[pl.ds(s * tt, tt)], buf_ref.at[s], cw_ref, cb_ref, xx_ref.at[s], ti == 0, tt)
          for s in range(n_seq)]
    xc = xc[0] if n_seq == 1 else jnp.concatenate(xc, axis=0)
    xcb = xc.astype(bf16)
    r = jax.nn.sigmoid(_dot(xcb, wa_ref[...]) + ba_ref[...])
    i = jax.nn.sigmoid(_dot(xcb, wx_ref[...]) + bx_ref[...])
    log_a = (-LRU_C) * r * _softplus(-lam_ref[...])
    a = jnp.exp(log_a)
    a_ref[...] = a
    u_ref[...] = jnp.sqrt(1.0 - a * a) * (i * xc)

    @pl.when(ti == 0)
    def _():
        h_ref[...] = h0_ref[...]

    row = lax.broadcasted_iota(jnp.int32, (SUBLANES, W_B), 0)

    def tile(start, h):
        rows = pl.ds(pl.multiple_of(start, SUBLANES), SUBLANES)
        a = a_ref[rows, :]
        u = u_ref[rows, :]
        for sh in (1, 2, 4):
            keep = row >= sh
            u = jnp.where(keep, a * pltpu.roll(u, sh, 0) + u, u)
            a = jnp.where(keep, a * pltpu.roll(a, sh, 0), a)
        hh = a * h + u
        y_ref[rows, :] = hh * gate_ref[rows, :]
        return hh[SUBLANES - 1:SUBLANES, :]

    for s in range(n_seq):
        if tt == SUBLANES:
            h = tile(s * tt, h_ref[s])
        else:
            h = lax.fori_loop(0, tt // SUBLANES, lambda k, h, s=s: tile(s * tt + k * SUBLANES, h), h_ref[s],
                              unroll=LRU_SCAN_UNROLL)
        h_ref[s] = h
        hf_ref[s] = h


LRU_SCAN_UNROLL = 4


def _lru(proj, buf, h0, cw, cb, wa, ba, wx, bx, lam, layer, b, seq):
    n_seq, tt = (1, min(seq, 512)) if seq >= LANES else (LANES // seq, seq)
    nt = seq // tt
    rows = n_seq * tt
    small = lambda shape: pl.BlockSpec(shape, lambda i, j: (0,) * len(shape))
    return pl.pallas_call(
        functools.partial(_lru_body, n_seq=n_seq, tt=tt),
        grid=(b // n_seq, nt),
        in_specs=[
            pl.BlockSpec((rows, 512), lambda i, j: (i * nt + j, COL_XB)),
            pl.BlockSpec((rows, 512), lambda i, j: (i * nt + j, COL_GB)),
            pl.BlockSpec((None, n_seq, CONV_W - 1, W_B), lambda i, j: (layer, i, 0, 0)),
            pl.BlockSpec((None, n_seq, 1, W_B), lambda i, j: (layer, i, 0, 0)),
            small((CONV_W, W_B)), small((1, W_B)), small((W_B, W_B)), small((1, W_B)), small((W_B, W_B)),
            small((1, W_B)), small((1, W_B)),
        ],
        out_specs=[
            pl.BlockSpec((rows, 512), lambda i, j: (i * nt + j, 0)),
            pl.BlockSpec((n_seq, 1, W_B), lambda i, j: (i, 0, 0)),
        ],
        out_shape=[jax.ShapeDtypeStruct((b * seq, W_B), f32), jax.ShapeDtypeStruct((b, 1, W_B), f32)],
        scratch_shapes=[pltpu.VMEM((n_seq, tt + SUBLANES, W_B), f32), pltpu.VMEM((rows, W_B), f32),
                        pltpu.VMEM((rows, W_B), f32), pltpu.VMEM((n_seq, 1, W_B), f32)],
        compiler_params=_cparams(("parallel", "arbitrary")),
        name="lru",
    )(proj, proj, buf, h0, cw, cb, wa, ba, wx, bx, lam)


def _cmix_body(u_ref, v_ref, g_ref, w_ref, b_ref, y_ref, *maybe_vn_ref, n_chunks):
    vn = _rms(v_ref[...], g_ref[...])
    for vn_ref in maybe_vn_ref:
        vn_ref[...] = vn
    r = lax.broadcasted_iota(jnp.int32, (CHUNK_C, CHUNK_C), 0)
    c = lax.broadcasted_iota(jnp.int32, (CHUNK_C, CHUNK_C), 1)
    for g in range(G_C):
        sl = slice(g * LANES, (g + 1) * LANES)
        wg = jnp.where(c <= r, w_ref[g], 0.0).astype(bf16)
        bias = b_ref[:, g:g + 1]
        for ch in range(n_chunks):
            rows = slice(ch * CHUNK_C, (ch + 1) * CHUNK_C)
            mixed = _dot(wg, vn[rows, sl].astype(bf16)) + bias
            y_ref[rows, sl] = u_ref[rows, sl] * mixed


def _cmix(proj, g, w, bias, emit_vn):
    t = proj.shape[0]
    tc = min(t, 512)
    n_out = 2 if emit_vn else 1
    return pl.pallas_call(
        functools.partial(_cmix_body, n_chunks=tc // CHUNK_C),
        grid=(t // tc,),
        in_specs=[
            pl.BlockSpec((tc, 512), lambda i: (i, COL_UC)),
            pl.BlockSpec((tc, 512), lambda i: (i, COL_VC)),
            pl.BlockSpec((1, W_C), lambda i: (0, 0)),
            pl.BlockSpec((G_C, CHUNK_C, CHUNK_C), lambda i: (0, 0, 0)),
            pl.BlockSpec((CHUNK_C, G_C), lambda i: (0, 0)),
        ],
        out_specs=[pl.BlockSpec((tc, 512), lambda i: (i, 0))] * n_out,
        out_shape=[jax.ShapeDtypeStruct((t, W_C), f32)] * n_out,
        compiler_params=_cparams(("parallel",)),
        name="cmix",
    )(proj, proj, g, w, bias)


def _ssd_body(xbc_ref, z_ref, dt_ref, buf_ref, h0_ref, cw_ref, cb_ref, dtb_ref, alog_ref, d_ref, g_ref, hf_all_ref,
              y_ref, hf_ref, xx_ref, xc_ref, st_ref, *, n_seq, rows, n_chunk):
    del hf_all_ref
    ci = pl.program_id(1)
    q = SSD_CHUNK

    @pl.when(ci == 0)
    def _():
        st_ref[...] = h0_ref[...]

    per_seq = rows * n_chunk if n_seq == 1 else rows
    xbc = [_conv_tile(xbc_ref.at[pl.ds(s * per_seq, per_seq)], buf_ref.at[s], cw_ref, cb_ref, xx_ref.at[s], ci == 0,
                      per_seq) for s in range(n_seq)]
    xbc = xbc[0] if n_seq == 1 else jnp.concatenate(xbc, axis=0)
    xc_ref[...] = xbc * jax.nn.sigmoid(xbc)
    dtx = dt_ref[...] + dtb_ref[...]
    dt_all = jnp.maximum(dtx, 0.0) + jnp.log(1.0 + jnp.exp(-jnp.abs(dtx)))
    a_neg = -jnp.exp(alog_ref[...])
    ri = lax.broadcasted_iota(jnp.int32, (q, q), 0)
    cj = lax.broadcasted_iota(jnp.int32, (q, q), 1)
    causal = cj <= ri
    if n_seq > 1:
        causal = causal & ((ri // rows) == (cj // rows))
    exact_dot = functools.partial(jnp.dot, preferred_element_type=f32, precision=lax.Precision.HIGHEST)
    lane = lax.broadcasted_iota(jnp.int32, (q, LANES), 1)
    low = lane < P_D
    row_seq = lax.broadcasted_iota(jnp.int32, (q, LANES), 0) // rows
    rlow = lax.broadcasted_iota(jnp.int32, (LANES, N_D), 0) < P_D
    for c in range(n_chunk):
        cr = slice(c * q, (c + 1) * q)
        xbc = xc_ref[cr, :]
        dt = dt_all[cr]
        xs = xbc[:, :W_D]
        cum = exact_dot(jnp.where(causal, 1.0, 0.0).astype(f32), dt * a_neg)
        cum_t = cum.T
        if n_seq > 1:
            last_rows = exact_dot(jnp.where(cj == (ri // rows) * rows + (rows - 1), 1.0, 0.0).astype(f32), cum)
        else:
            last_rows = cum[q - 1:q, :]
        y_parts = []
        for g in range(G_D):
            bm = xbc[:, W_D + g * N_D:W_D + (g + 1) * N_D].astype(bf16)
            cm_f = xbc[:, W_D + (G_D + g) * N_D:W_D + (G_D + g + 1) * N_D]
            cm = cm_f.astype(bf16)
            scores = _dot_nt(cm, bm)
            for k in range(H_D // G_D // 2):
                h0 = g * (H_D // G_D) + 2 * k
                sl = slice(h0 * P_D, (h0 + 2) * P_D)
                col = lambda a, h: a[:, h:h + 1]
                pair = lambda a: jnp.where(low, col(a, h0), col(a, h0 + 1))
                xdt = xs[:, sl] * pair(dt)
                xdt_b = xdt.astype(bf16)
                yd = []
                for h in (h0, h0 + 1):
                    seg = col(cum, h) - cum_t[h:h + 1, :]
                    m = scores * jnp.exp(jnp.where(causal, seg, -jnp.inf))
                    yd.append(_dot(m.astype(bf16), xdt_b))
                xw = xdt * jnp.exp(pair(last_rows - cum))
                offs = []
                for s in range(n_seq):
                    rs = slice(s * rows, (s + 1) * rows)
                    st = st_ref[s, sl, :]
                    offs.append(_dot_nt(cm if n_seq == 1 else cm_f[rs].astype(bf16), st.astype(bf16)))
                    last = cum[(s + 1) * rows - 1:(s + 1) * rows, :]
                    dec = jnp.exp(jnp.where(rlow, last[:, h0:h0 + 1], last[:, h0 + 1:h0 + 2]))
                    xw_s = xw if n_seq == 1 else jnp.where(row_seq == s, xw, 0.0)
                    st_ref[s, sl, :] = dec * st + _dot_tn(xw_s.astype(bf16), bm)
                y_off = (offs[0] if n_seq == 1 else jnp.concatenate(offs, axis=0)) * jnp.exp(pair(cum))
                y_parts.append(jnp.where(low, yd[0], yd[1]) + y_off + d_ref[:, sl] * xs[:, sl])
        y = jnp.concatenate(y_parts, axis=1) * z_ref[cr, :]
        y_ref[cr, :] = _rms(y, g_ref[...])
    hf_ref[...] = st_ref[...]


SSD_CHUNKS_PER_STEP = 4


def _ssd(proj, dt_raw, buf, h0, cw, cb, dtb, alog, dvec, g, hf_all, layer, layer_out, b, seq):
    q = SSD_CHUNK
    if seq >= q:
        n_seq, rows = 1, q
        n_chunk = SSD_CHUNKS_PER_STEP if (seq // q) % SSD_CHUNKS_PER_STEP == 0 else 1
    else:
        n_seq, rows, n_chunk = q // seq, seq, 1
    blk = q * n_chunk
    nc = seq // blk if n_seq == 1 else 1
    small = lambda shape: pl.BlockSpec(shape, lambda i, j: (0,) * len(shape))
    return pl.pallas_call(
        functools.partial(_ssd_body, n_seq=n_seq, rows=rows, n_chunk=n_chunk),
        grid=(b // n_seq, nc),
        in_specs=[
            pl.BlockSpec((blk, XBC_D), lambda i, j: (i * nc + j, COL_XBC)),
            pl.BlockSpec((blk, 512), lambda i, j: (i * nc + j, COL_ZD)),
            pl.BlockSpec((blk, LANES), lambda i, j: (i * nc + j, 0)),
            pl.BlockSpec((None, n_seq, CONV_W - 1, XBC_D), lambda i, j: (layer, i, 0, 0)),
            pl.BlockSpec((None, n_seq, H_D * P_D, N_D), lambda i, j: (layer, i, 0, 0)),
            small((CONV_W, XBC_D)), small((1, XBC_D)), small((1, LANES)), small((1, LANES)), small((1, W_D)),
            small((1, W_D)), pl.BlockSpec(memory_space=pl.ANY),
        ],
        out_specs=[
            pl.BlockSpec((blk, W_D), lambda i, j: (i * nc + j, 0)),
            pl.BlockSpec((None, n_seq, H_D * P_D, N_D), lambda i, j: (layer_out, i, 0, 0)),
        ],
        out_shape=[jax.ShapeDtypeStruct((b * seq, W_D), f32), jax.ShapeDtypeStruct(hf_all.shape, f32)],
        input_output_aliases={11: 1},
        scratch_shapes=[pltpu.VMEM((n_seq, blk // n_seq + SUBLANES, XBC_D), f32), pltpu.VMEM((blk, XBC_D), f32),
                        pltpu.VMEM((n_seq, H_D * P_D, N_D), f32)],
        compiler_params=_cparams(("parallel", "arbitrary")),
        name="ssd",
    )(proj, proj, dt_raw, buf, h0, cw, cb, dtb, alog, dvec, g, hf_all)


def _merge_body(x_ref, g_ref, ya_ref, yb_ref, yc_ref, yd_ref, wg_ref, wb_ref, wo_ref, o_ref):
    x = x_ref[...]
    hn = _rms(x, g_ref[...]).astype(bf16)
    m = None
    for k, y_ref in enumerate((ya_ref, yb_ref, yc_ref, yd_ref)):
        gate = jax.nn.sigmoid(_dot(hn, wg_ref[:, k * D_MODEL:(k + 1) * D_MODEL]))
        t = gate * _dot(y_ref[...].astype(bf16), wb_ref[k])
        m = t if m is None else m + t
    o_ref[...] = x + _dot(m.astype(bf16), wo_ref[...])


def _merge(x, g, ya, yb, yc, yd, wg, wb, wo, layer):
    t = x.shape[0]
    tm = min(t, ROW_TILE)
    row = lambda w: pl.BlockSpec((tm, w), lambda i: (i, 0))
    return pl.pallas_call(
        _merge_body,
        grid=(t // tm,),
        in_specs=[row(D_MODEL), _resident((1, D_MODEL), lambda i: (0, 0)), row(512), row(512), row(512), row(512),
                  _resident((None, D_MODEL, N_BRANCH * D_MODEL), lambda i: (layer, 0, 0)),
                  _resident((None, N_BRANCH, 512, D_MODEL), lambda i: (layer, 0, 0, 0)),
                  _resident((None, D_MODEL, D_MODEL), lambda i: (layer, 0, 0))],
        out_specs=row(D_MODEL),
        out_shape=jax.ShapeDtypeStruct((t, D_MODEL), f32),
        compiler_params=_cparams(("parallel",)),
        name="merge",
    )(x, g, ya, yb, yc, yd, wg, wb, wo)


def _memkv_body(m_ref, wk_ref, wv_ref, g_ref, k_ref, v_ref, kt_ref, vt_ref):
    mb = m_ref[...].astype(bf16)
    rows = mb.shape[0]
    kk = _dot(mb, wk_ref[...])
    vv = _dot(mb, wv_ref[...])
    v_ref[...] = vv
    for h in range(XH):
        sl = slice(h * XDH, (h + 1) * XDH)
        kh = _rms(kk[:, sl], g_ref[...])
        k_ref[:, sl] = kh
        kt_ref[pl.ds(h, rows, stride=XH), :] = kh
        vt_ref[pl.ds(h, rows, stride=XH), :] = vv[:, sl]


def _memkv(mem, wk, wv, g):
    r = mem.shape[0]
    tm = min(r, 512)
    return pl.pallas_call(
        _memkv_body,
        grid=(DEPTH, r // tm),
        in_specs=[pl.BlockSpec((tm, D_MODEL), lambda l, i: (i, 0)),
                  pl.BlockSpec((None, D_MODEL, XW), lambda l, i: (l, 0, 0)),
                  pl.BlockSpec((None, D_MODEL, XW), lambda l, i: (l, 0, 0)),
                  pl.BlockSpec((None, 1, XDH), lambda l, i: (l, 0, 0))],
        out_specs=[pl.BlockSpec((None, tm, XW), lambda l, i: (l, i, 0))] * 2
        + [pl.BlockSpec((None, tm * XH, XDH), lambda l, i: (l, i, 0))] * 2,
        out_shape=[jax.ShapeDtypeStruct((DEPTH, r, XW), f32)] * 2 + [jax.ShapeDtypeStruct((DEPTH, r * XH, XDH), f32)] * 2,
        compiler_params=_cparams(("parallel", "parallel")),
        name="memkv",
    )(mem, wk, wv, g)


def _cross_body(x_ref, g_ref, wq_ref, qg_ref, k_ref, v_ref, wo_ref, o_ref, *, n_seq, rows, tall):
    x = x_ref[...]
    qx = _dot(_rms(x, g_ref[...]).astype(bf16), wq_ref[...])
    qn = [_rms(qx[:, h * XDH:(h + 1) * XDH], qg_ref[...]) for h in range(XH)]

    def attend(sc, v):
        p = jnp.exp(sc - jnp.max(sc, axis=-1, keepdims=True))
        return _dot(p.astype(bf16), v) * (1.0 / jnp.sum(p, axis=-1, keepdims=True))

    if tall:
        r = lax.broadcasted_iota(jnp.int32, (XH * rows, N_MEM * XH), 0)
        c = lax.broadcasted_iota(jnp.int32, (XH * rows, N_MEM * XH), 1)
        same_head = (c % XH) == (r // rows)
        outs = []
        for s in range(n_seq):
            qs = jnp.concatenate([q[s * rows:(s + 1) * rows] for q in qn], axis=0)
            sc = _dot_nt(qs.astype(bf16), k_ref[s].astype(bf16)) * (XDH ** -0.5)
            o = attend(jnp.where(same_head, sc, -jnp.inf), v_ref[s].astype(bf16))
            outs.append(jnp.concatenate([o[h * rows:(h + 1) * rows] for h in range(XH)], axis=1))
        ox = jnp.concatenate(outs, axis=0)
    else:
        heads = []
        for h in range(XH):
            sl = slice(h * XDH, (h + 1) * XDH)
            outs = []
            for s in range(n_seq):
                sc = _dot_nt(qn[h][s * rows:(s + 1) * rows].astype(bf16), k_ref[s, :, sl].astype(bf16)) * (XDH ** -0.5)
                outs.append(attend(sc, v_ref[s, :, sl].astype(bf16)))
            heads.append(outs[0] if n_seq == 1 else jnp.concatenate(outs, axis=0))
        ox = jnp.concatenate(heads, axis=1)
    o_ref[...] = x + _dot(ox.astype(bf16), wo_ref[...])


def _cross(x, g, wq, qg, mem_k, mem_v, wo, layer, seq, tall):
    t = x.shape[0]
    weight = lambda r, c: pl.BlockSpec((None, r, c), lambda i: (layer, 0, 0))
    if seq >= 512:
        n_seq, rows = 1, 512
        per = seq // rows
        mem_idx = lambda i: (layer, i // per, 0, 0)
    else:
        n_seq, rows = 8, seq
        mem_idx = lambda i: (layer, i, 0, 0)
    tm = n_seq * rows
    small = lambda shape: pl.BlockSpec(shape, lambda i: (0,) * len(shape))
    mem_block = (None, n_seq) + tuple(mem_k.shape[2:])
    return pl.pallas_call(
        functools.partial(_cross_body, n_seq=n_seq, rows=rows, tall=tall),
        grid=(t // tm,),
        in_specs=[pl.BlockSpec((tm, D_MODEL), lambda i: (i, 0)), small((1, D_MODEL)), weight(D_MODEL, XW),
                  small((1, XDH)), pl.BlockSpec(mem_block, mem_idx), pl.BlockSpec(mem_block, mem_idx),
                  weight(XW, D_MODEL)],
        out_specs=pl.BlockSpec((tm, D_MODEL), lambda i: (i, 0)),
        out_shape=jax.ShapeDtypeStruct((t, D_MODEL), f32),
        compiler_params=_cparams(("parallel",)),
        name="cross",
    )(x, g, wq, qg, mem_k, mem_v, wo)


def _ffn_body(x_ref, g_ref, wg_ref, wu_ref, wo_ref, o_ref):
    x = x_ref[...]
    hn = _rms(x, g_ref[...]).astype(bf16)
    gate = _dot(hn, wg_ref[...])
    up = _dot(hn, wu_ref[...])
    act = (gate * jax.nn.sigmoid(gate) * up).astype(bf16)
    o_ref[...] = x + _dot(act, wo_ref[...])


def _ffn(x, g, w_in, w_out, layer):
    t = x.shape[0]
    tm = min(t, ROW_TILE)
    return pl.pallas_call(
        _ffn_body,
        grid=(t // tm,),
        in_specs=[pl.BlockSpec((tm, D_MODEL), lambda i: (i, 0)),
                  _resident((1, D_MODEL), lambda i: (0, 0)),
                  _resident((None, D_MODEL, D_FF), lambda i: (layer, 0, 0)),
                  _resident((None, D_MODEL, D_FF), lambda i: (layer, 0, 1)),
                  _resident((None, D_FF, D_MODEL), lambda i: (layer, 0, 0))],
        out_specs=pl.BlockSpec((tm, D_MODEL), lambda i: (i, 0)),
        out_shape=jax.ShapeDtypeStruct((t, D_MODEL), f32),
        compiler_params=_cparams(("parallel",)),
        name="ffn",
    )(x, g, w_in, w_in, w_out)


def _rope_tables(pos):
    half = DH_A // 2
    inv = ROPE_THETA ** (-jnp.arange(half, dtype=f32) / half)
    ang = pos.astype(f32)[:, None] * inv[None, :]
    cos, sin = jnp.cos(ang), jnp.sin(ang)
    cos_h = jnp.concatenate([cos, cos], axis=-1)
    sin_h = jnp.concatenate([-sin, sin], axis=-1)
    reps = W_A // DH_A
    return jnp.tile(cos_h, (1, reps)), jnp.tile(sin_h, (1, reps))


def _block_diag(w):
    nb, a, _ = w.shape
    eye = jnp.eye(nb, dtype=w.dtype)
    return (eye[:, None, :, None] * w[:, :, None, :]).reshape(nb * a, nb * a)


def _stacked_weights(P):
    w_in = P['w_in']
    return {
        'w_main': w_in[:, :, :N_MAIN].astype(bf16), 'w_gates': w_in[:, :, N_MAIN + H_D:].astype(bf16),
        'w_dt': jnp.pad(w_in[:, :, N_MAIN:N_MAIN + H_D], ((0, 0), (0, 0), (0, LANES - H_D))).astype(bf16),
        'w_branch': P['w_branch'].astype(bf16), 'w_o': P['w_o'].astype(bf16),
        'w_xq': P['w_xq'].astype(bf16), 'w_xo': P['w_xo'].astype(bf16),
        'w_ffn_in': P['w_ffn_in'].astype(bf16), 'w_ffn_out': P['w_ffn_out'].astype(bf16),
    }


def _layer_params(l, P, W):
    lp = dict(W)
    lp.update({
        'norm_mix': P['norm_mix'][l][None],
        'q_gain': jnp.tile(P['q_norm'][l], 2)[None],
        'k_gain': jnp.tile(P['k_norm'][l], 2)[None],
        'lambda_qk': P['lambda_qk'][l],
        'subln': P['subln'][l].reshape(1, W_A),
        'lru_cw': P['lru_conv_w'][l], 'lru_cb': P['lru_conv_b'][l][None],
        'lru_wa': _block_diag(P['lru_wa'][l]).astype(bf16), 'lru_ba': P['lru_ba'][l][None],
        'lru_wx': _block_diag(P['lru_wx'][l]).astype(bf16), 'lru_bx': P['lru_bx'][l][None],
        'lru_lambda': P['lru_lambda'][l][None],
        'sgu_norm': P['sgu_norm'][l][None],
        'sgu_w': P['sgu_w'][l], 'sgu_b': P['sgu_b'][l],
        'ssd_cw': P['ssd_conv_w'][l], 'ssd_cb': P['ssd_conv_b'][l][None],
        'ssd_dtb': jnp.pad(P['ssd_dt_bias'][l], (0, LANES - H_D))[None],
        'ssd_alog': jnp.pad(P['ssd_a_log'][l], (0, LANES - H_D))[None],
        'ssd_d': jnp.repeat(P['ssd_d'][l], P_D)[None],
        'ssd_norm': P['ssd_norm'][l][None],
        'norm_cross': P['norm_cross'][l][None],
        'xq_norm': P['xq_norm'][l][None],
        'norm_ffn': P['norm_ffn'][l][None],
    })
    return lp


def _sgu_weights(lp, seq):
    w, bias = lp['sgu_w'], lp['sgu_b']
    if seq >= CHUNK_C:
        return w, bias.T
    reps = CHUNK_C // seq
    eye = jnp.eye(reps, dtype=w.dtype)
    wk = (eye[None, :, None, :, None] * w[:, None, :seq, None, :seq]).reshape(G_C, CHUNK_C, CHUNK_C)
    return wk, jnp.tile(bias[:, :seq], (1, reps)).T


def _head_bd():
    i = jnp.arange(LANES) // DH_A
    return jnp.where(i[:, None] == i[None, :], 1.0 / DH_A, 0.0).astype(bf16)


def _layer(l, lp, x, b, seq, rope_tab, states, sl, mem_k, mem_v, ml, mem_tall, stacked, paged=None):
    lru_h0, lru_buf, ssm_h0, ssm_buf = states
    v_all, ssm_all = stacked
    lam_init = 0.8 - 0.6 * math.exp(-0.3 * l)
    proj, dt_raw = _in_proj(x, lp['norm_mix'], lp['w_main'], lp['w_dt'], l)
    cos, sin = rope_tab
    act = f32 if paged is not None else bf16
    q_scale = DH_A ** -0.5 * (1.0 if paged is not None else math.log2(math.e))
    q, k, kb, vb, v_all = _qk_prep(proj, cos, sin, lp['q_gain'], lp['k_gain'], _head_bd(), act, q_scale, v_all, l)
    if paged is None:
        y_a = _attn_prompt(lp['lambda_qk'], q, kb, vb, lp['subln'], lam_init, b, seq)
    else:
        page_table, cache_kt, cache_v2 = paged
        y_a = _attn_sample(page_table, lp['lambda_qk'], q, kb, vb, lp['subln'], cache_kt, cache_v2, l, lam_init, b, seq)
    y_b, lru_h = _lru(proj, lru_buf, lru_h0, lp['lru_cw'], lp['lru_cb'], lp['lru_wa'], lp['lru_ba'], lp['lru_wx'],
                      lp['lru_bx'], lp['lru_lambda'], sl, b, seq)
    sgu_w, sgu_b = _sgu_weights(lp, seq)
    y_c, *vc = _cmix(proj, lp['sgu_norm'], sgu_w, sgu_b, emit_vn=paged is not None)
    vc = vc[0] if vc else None
    y_d, ssm_all = _ssd(proj, dt_raw, ssm_buf, ssm_h0, lp['ssd_cw'], lp['ssd_cb'], lp['ssd_dtb'], lp['ssd_alog'],
                        lp['ssd_d'], lp['ssd_norm'], ssm_all, sl, l, b, seq)
    assert ml == l
    x = _merge(x, lp['norm_mix'], y_a, y_b, y_c, y_d, lp['w_gates'], lp['w_branch'], lp['w_o'], l)
    x = _cross(x, lp['norm_cross'], lp['w_xq'], lp['xq_norm'], mem_k, mem_v, lp['w_xo'], ml, seq, mem_tall)
    x = _ffn(x, lp['norm_ffn'], lp['w_ffn_in'], lp['w_ffn_out'], l)
    p3 = proj.reshape(b, seq, N_MAIN)
    assert seq >= CONV_W - 1
    new_lru_buf = p3[:, seq - (CONV_W - 1):, COL_XB * 512:(COL_XB + 1) * 512]
    new_ssm_buf = p3[:, seq - (CONV_W - 1):, COL_XBC * 1024:(COL_XBC + 1) * 1024]
    return x, (k, lru_h, new_lru_buf, new_ssm_buf, vc), (v_all, ssm_all)


def kernel(x_prompt, x_sample, mem_prompt, cache_attn_k, cache_attn_v, page_table, cache_mem_k, cache_mem_v, state_lru, state_lru_conv, state_ssm, state_ssm_conv, norm_mix, w_in, q_norm, k_norm, lambda_qk, subln, lru_conv_w, lru_conv_b, lru_wa, lru_ba, lru_wx, lru_bx, lru_lambda, sgu_norm, sgu_w, sgu_b, ssd_conv_w, ssd_conv_b, ssd_dt_bias, ssd_a_log, ssd_d, ssd_norm, w_branch, w_o, norm_cross, w_xq, w_xk, w_xv, xq_norm, xk_norm, w_xo, norm_ffn, w_ffn_in, w_ffn_out):
    P = dict(norm_mix=norm_mix, w_in=w_in, q_norm=q_norm, k_norm=k_norm, lambda_qk=lambda_qk, subln=subln,
             lru_conv_w=lru_conv_w, lru_conv_b=lru_conv_b, lru_wa=lru_wa, lru_ba=lru_ba, lru_wx=lru_wx, lru_bx=lru_bx,
             lru_lambda=lru_lambda, sgu_norm=sgu_norm, sgu_w=sgu_w, sgu_b=sgu_b, ssd_conv_w=ssd_conv_w,
             ssd_conv_b=ssd_conv_b, ssd_dt_bias=ssd_dt_bias, ssd_a_log=ssd_a_log, ssd_d=ssd_d, ssd_norm=ssd_norm,
             w_branch=w_branch, w_o=w_o, norm_cross=norm_cross, w_xq=w_xq, xq_norm=xq_norm, w_xo=w_xo,
             norm_ffn=norm_ffn, w_ffn_in=w_ffn_in, w_ffn_out=w_ffn_out)
    bp, lp_, _ = x_prompt.shape
    bs, ls, _ = x_sample.shape
    n_pages = page_table.shape[1]
    past_len = n_pages * PAGE_SIZE
    n_phys = cache_attn_k.shape[1]

    rope_p = _rope_tables(jnp.arange(lp_))
    cos_s, sin_s = _rope_tables(past_len + jnp.arange(ls))
    rope_s = (jnp.tile(cos_s, (bs, 1)), jnp.tile(sin_s, (bs, 1)))
    cache_kt = jnp.transpose(cache_attn_k, (0, 1, 3, 4, 2))
    cache_v2 = cache_attn_v.reshape(DEPTH, n_phys, PAGE_SIZE * H_A, DV_A)
    mem_k_s = cache_mem_k.reshape(DEPTH, bs, N_MEM * XH, XDH)
    mem_v_s = cache_mem_v.reshape(DEPTH, bs, N_MEM * XH, XDH)

    pmk, pmv, pmk_t, pmv_t = _memkv(mem_prompt.reshape(bp * N_MEM, D_MODEL), w_xk.astype(bf16), w_xv.astype(bf16),
                                    xk_norm[:, None, :])

    yp = x_prompt.reshape(bp * lp_, D_MODEL)
    ys = x_sample.reshape(bs * ls, D_MODEL)
    zeros = lambda *s: jnp.zeros(s, f32)
    states_p = (zeros(1, bp, 1, W_B), zeros(1, bp, CONV_W - 1, W_B), zeros(1, bp, H_D * P_D, N_D),
                zeros(1, bp, CONV_W - 1, XBC_D))
    states_s = (state_lru[:, :, None, :], state_lru_conv, state_ssm.reshape(DEPTH, bs, H_D * P_D, N_D), state_ssm_conv)
    pmk4 = pmk.reshape(DEPTH, bp, N_MEM, XW)
    pmv4 = pmv.reshape(DEPTH, bp, N_MEM, XW)
    stacked_p = (zeros(DEPTH, bp * lp_ * H_A, DV_A), zeros(DEPTH, bp, H_D * P_D, N_D))
    stacked_s = (zeros(DEPTH, bs * ls * H_A, DV_A), zeros(DEPTH, bs, H_D * P_D, N_D))
    p_states, s_states = [], []
    stacked_w = _stacked_weights(P)
    for l in range(DEPTH):
        lp = _layer_params(l, P, stacked_w)
        yp, sp, stacked_p = _layer(l, lp, yp, bp, lp_, rope_p, states_p, 0, pmk4, pmv4, l, False, stacked_p)
        p_states.append(sp)
        ys, ss, stacked_s = _layer(l, lp, ys, bs, ls, rope_s, states_s, l, mem_k_s, mem_v_s, l, True, stacked_s,
                                   paged=(page_table, cache_kt, cache_v2))
        s_states.append(ss)

    def stack(states, idx, shape):
        return jnp.stack([s[idx] for s in states]).reshape((DEPTH,) + shape)

    return (
        yp.reshape(bp, lp_, D_MODEL), ys.reshape(bs, ls, D_MODEL),
        stack(p_states, 0, (bp, lp_, 2 * H_A, DH_A)), stacked_p[0].reshape(DEPTH, bp, lp_, H_A, DV_A),
        pmk_t.reshape(DEPTH, bp, N_MEM, XH, XDH), pmv_t.reshape(DEPTH, bp, N_MEM, XH, XDH),
        stack(p_states, 1, (bp, W_B)), stack(p_states, 2, (bp, CONV_W - 1, W_B)),
        stacked_p[1].reshape(DEPTH, bp, H_D, P_D, N_D), stack(p_states, 3, (bp, CONV_W - 1, XBC_D)),
        stack(s_states, 0, (bs, ls, 2 * H_A, DH_A)), stacked_s[0].reshape(DEPTH, bs, ls, H_A, DV_A),
        stack(s_states, 1, (bs, W_B)), stack(s_states, 2, (bs, CONV_W - 1, W_B)),
        stacked_s[1].reshape(DEPTH, bs, H_D, P_D, N_D), stack(s_states, 3, (bs, CONV_W - 1, XBC_D)),
        stack(s_states, 4, (bs, ls, W_C)),
    )
```
